```python
import jax, jax.numpy as jnp
from jax import lax
import numpy as np

D_MODEL = 1024
BATCH = 32
SEQ = 2048
DEPTH = 4
DEC_BATCH = 32
DEC_SEQ = 32
PAST_LEN = 2048

CHUNK = 64
D_MIX = D_MODEL
N_HEADS_ATTN = 8
HEAD_DIM = 64
D_ATTN = N_HEADS_ATTN * HEAD_DIM
D_CONV = D_MIX - D_ATTN
CONV_WIDTH = 3
D_FF = 2816
FFN_CONV_WIDTH = 3
Q_BLOCK = 128
ALPHA = (2 * DEPTH) ** 0.25
BETA = (8 * DEPTH) ** -0.25
LN_EPS = 1e-5
D_IN = 3 * D_ATTN + N_HEADS_ATTN + 3 * D_CONV
SPLITS = [D_ATTN, 2 * D_ATTN, 3 * D_ATTN, 3 * D_ATTN + N_HEADS_ATTN,
          3 * D_ATTN + N_HEADS_ATTN + D_CONV, 3 * D_ATTN + N_HEADS_ATTN + 2 * D_CONV]

kernel_name = "fox_shortconv_convffn_deepnorm_stream_step"


def _layer_norm(x, g, b):
    xf = x.astype(jnp.float32)
    mu = jnp.mean(xf, axis=-1, keepdims=True)
    var = jnp.mean(jnp.square(xf - mu), axis=-1, keepdims=True)
    y = (xf - mu) * lax.rsqrt(var + LN_EPS) * g.astype(jnp.float32) + b.astype(jnp.float32)
    return y.astype(x.dtype)


def _causal_dwconv(u_ext, w, b, out_len):
    y = b
    for i in range(w.shape[0]):
        y = y + w[i] * u_ext[:, i:i + out_len]
    return y


def _fox_attend(q, k, v, c_q, c_k, q_pos, k_pos):
    s = jnp.einsum('nqhd,nkhd->nhqk', q.astype(jnp.float32), k.astype(jnp.float32)) * (HEAD_DIM ** -0.5)
    s = s + jnp.transpose(c_q, (0, 2, 1))[:, :, :, None] - jnp.transpose(c_k, (0, 2, 1))[:, :, None, :]
    mask = k_pos[None, :] <= q_pos[:, None]
    s = jnp.where(mask[None, None], s, -jnp.inf)
    p = jax.nn.softmax(s, axis=-1)
    return jnp.einsum('nhqk,nkhd->nqhd', p.astype(v.dtype), v)


def _fox_prompt(q, k, v, logf):
    T = q.shape[1]
    c = jnp.cumsum(logf, axis=1)
    pos = jnp.arange(T)
    outs = []
    for blk in range(T // Q_BLOCK):
        lo, hi = blk * Q_BLOCK, (blk + 1) * Q_BLOCK
        outs.append(_fox_attend(q[:, lo:hi], k[:, :hi], v[:, :hi], c[:, lo:hi], c[:, :hi], pos[lo:hi], pos[:hi]))
    return jnp.concatenate(outs, axis=1)


def _fox_sample(q, k, v, logf, cache_k, cache_v, cache_logf):
    P, T = cache_k.shape[1], q.shape[1]
    k_all = jnp.concatenate([cache_k, k], axis=1)
    v_all = jnp.concatenate([cache_v, v], axis=1)
    c = jnp.cumsum(jnp.concatenate([cache_logf.astype(jnp.float32), logf], axis=1), axis=1)
    pos = jnp.arange(P + T)
    return _fox_attend(q, k_all, v_all, c[:, P:], c, pos[P:], pos)


def _layer(x, w_in, b_f, conv_w, conv_b, w_out, ln1_g, ln1_b,
           w_up, ffn_conv_w, ffn_conv_b, w_down, ln2_g, ln2_b, past=None):
    N, T, _ = x.shape
    proj = x @ w_in
    q, k, v, f_logit, gate_b, gate_c, h = jnp.split(proj, SPLITS, axis=-1)
    q = q.reshape(N, T, N_HEADS_ATTN, HEAD_DIM)
    k = k.reshape(N, T, N_HEADS_ATTN, HEAD_DIM)
    v = v.reshape(N, T, N_HEADS_ATTN, HEAD_DIM)
    logf = jax.nn.log_sigmoid((f_logit + b_f).astype(jnp.float32))
    u = gate_c * h
    if past is None:
        attn = _fox_prompt(q, k, v, logf)
        u_ext = jnp.pad(u, ((0, 0), (CONV_WIDTH - 1, 0), (0, 0)))
    else:
        cache_k, cache_v, cache_logf, mix_state, ffn_state = past
        attn = _fox_sample(q, k, v, logf, cache_k, cache_v, cache_logf)
        u_ext = jnp.concatenate([mix_state.astype(u.dtype), u], axis=1)
    conv_out = gate_b * _causal_dwconv(u_ext, conv_w, conv_b, T)
    mixed = jnp.concatenate([attn.reshape(N, T, D_ATTN), conv_out], axis=-1) @ w_out
    x = _layer_norm(ALPHA * x + mixed, ln1_g, ln1_b)

    up = x @ w_up
    g, val = jnp.split(up, [D_FF], axis=-1)
    if past is None:
        g_ext = jnp.pad(g, ((0, 0), (FFN_CONV_WIDTH - 1, 0), (0, 0)))
    else:
        g_ext = jnp.concatenate([ffn_state.astype(g.dtype), g], axis=1)
    a = jax.nn.silu(_causal_dwconv(g_ext, ffn_conv_w, ffn_conv_b, T)) * val
    x = _layer_norm(ALPHA * x + a @ w_down, ln2_g, ln2_b)
    new_state = (k, v, logf, u_ext[:, -(CONV_WIDTH - 1):], g_ext[:, -(FFN_CONV_WIDTH - 1):])
    return x, new_state


def setup_inputs(seed: int = 0) -> dict:
    key = jax.random.key(seed)
    ks = jax.random.split(key, 24)
    f32 = jnp.float32
    nrm = lambda k, shape, s: jax.random.normal(k, shape, f32) * s
    return {
        "x_prompt": nrm(ks[0], (BATCH, SEQ, D_MODEL), 1.0),
        "x_sample": nrm(ks[1], (DEC_BATCH, DEC_SEQ, D_MODEL), 1.0),
        "cache_k": nrm(ks[2], (DEPTH, DEC_BATCH, PAST_LEN, N_HEADS_ATTN, HEAD_DIM), 1.0),
        "cache_v": nrm(ks[3], (DEPTH, DEC_BATCH, PAST_LEN, N_HEADS_ATTN, HEAD_DIM), 1.0),
        "cache_logf": jax.nn.log_sigmoid(jax.random.uniform(ks[4], (DEPTH, DEC_BATCH, PAST_LEN, N_HEADS_ATTN), f32, 1.0, 4.0)
                                          + nrm(ks[5], (DEPTH, DEC_BATCH, PAST_LEN, N_HEADS_ATTN), 1.0)),
        "state_mix_conv": nrm(ks[6], (DEPTH, DEC_BATCH, CONV_WIDTH - 1, D_CONV), 1.0),
        "state_ffn_conv": nrm(ks[7], (DEPTH, DEC_BATCH, FFN_CONV_WIDTH - 1, D_FF), 1.0),
        "w_in": nrm(ks[8], (DEPTH, D_MODEL, D_IN), D_MODEL ** -0.5),
        "b_f": jax.random.uniform(ks[9], (DEPTH, N_HEADS_ATTN), f32, 1.0, 4.0),
        "conv_w": nrm(ks[10], (DEPTH, CONV_WIDTH, D_CONV), CONV_WIDTH ** -0.5),
        "conv_b": nrm(ks[11], (DEPTH, D_CONV), 0.01),
        "w_out": nrm(ks[12], (DEPTH, D_MIX, D_MODEL), BETA * D_MIX ** -0.5),
        "ln1_g": 1.0 + nrm(ks[13], (DEPTH, D_MODEL), 0.01),
        "ln1_b": nrm(ks[14], (DEPTH, D_MODEL), 0.01),
        "w_up": nrm(ks[15], (DEPTH, D_MODEL, 2 * D_FF), D_MODEL ** -0.5),
        "ffn_conv_w": nrm(ks[16], (DEPTH, FFN_CONV_WIDTH, D_FF), FFN_CONV_WIDTH ** -0.5),
        "ffn_conv_b": nrm(ks[17], (DEPTH, D_FF), 0.01),
        "w_down": nrm(ks[18], (DEPTH, D_FF, D_MODEL), BETA * D_FF ** -0.5),
        "ln2_g": 1.0 + nrm(ks[19], (DEPTH, D_MODEL), 0.01),
        "ln2_b": nrm(ks[20], (DEPTH, D_MODEL), 0.01),
    }


def reference(x_prompt, x_sample, cache_k, cache_v, cache_logf, state_mix_conv, state_ffn_conv,
              w_in, b_f, conv_w, conv_b, w_out, ln1_g, ln1_b,
              w_up, ffn_conv_w, ffn_conv_b, w_down, ln2_g, ln2_b):
    yp, ys = x_prompt, x_sample
    sp_list, ss_list = [], []
    for l in range(DEPTH):
        lw = (w_in[l], b_f[l], conv_w[l], conv_b[l], w_out[l], ln1_g[l], ln1_b[l],
              w_up[l], ffn_conv_w[l], ffn_conv_b[l], w_down[l], ln2_g[l], ln2_b[l])
        yp, sp = _layer(yp, *lw)
        ys, ss = _layer(ys, *lw, past=(cache_k[l], cache_v[l], cache_logf[l], state_mix_conv[l], state_ffn_conv[l]))
        sp_list.append(sp)
        ss_list.append(ss)
    k_prompt = jnp.stack([s[0] for s in sp_list])
    v_prompt = jnp.stack([s[1] for s in sp_list])
    logf_prompt = jnp.stack([s[2] for s in sp_list])
    mix_conv_prompt = jnp.stack([s[3] for s in sp_list])
    ffn_conv_prompt = jnp.stack([s[4] for s in sp_list])
    k_sample = jnp.stack([s[0] for s in ss_list])
    v_sample = jnp.stack([s[1] for s in ss_list])
    logf_sample = jnp.stack([s[2] for s in ss_list])
    mix_conv_sample = jnp.stack([s[3] for s in ss_list])
    ffn_conv_sample = jnp.stack([s[4] for s in ss_list])
    return (yp, ys, k_prompt, v_prompt, logf_prompt, mix_conv_prompt, ffn_conv_prompt,
            k_sample, v_sample, logf_sample, mix_conv_sample, ffn_conv_sample)
```

```python
import functools

import jax
import jax.numpy as jnp
from jax import lax
from jax.experimental import pallas as pl
from jax.experimental.pallas import tpu as pltpu

F32 = jnp.float32
BF16 = jnp.bfloat16

LANES = 128
SUBLANES = 8
LN_EPS = 1e-5
ROW_TILE = 512
Q_TILE = 256
FF_CHUNK = 256
VMEM_LIMIT = 56 * 1024 * 1024


def _dot(a, b):
    return jnp.dot(a, b, preferred_element_type=F32)


def _dot_nt(a, b):
    return lax.dot_general(a, b, (((1,), (1,)), ((), ())), preferred_element_type=F32)


def _stride_cumsum(x, s):
    rows = x.shape[0]
    row = lax.broadcasted_iota(jnp.int32, x.shape, 0)
    d = s
    while d < rows:
        x = x + jnp.where(row >= d, pltpu.roll(x, d, axis=0), 0.0)
        d *= 2
    return x


def _split3(v, nh):
    hi = v.astype(BF16).astype(F32)
    r1 = v - hi
    mid = r1.astype(BF16).astype(F32)
    lo = (r1 - mid).astype(BF16).astype(F32)
    lane = lax.broadcasted_iota(jnp.int32, v.shape, 1)
    out = jnp.where(lane < nh, hi, jnp.where(lane < 2 * nh, mid, jnp.where(lane < 3 * nh, lo, 0.0)))
    return out.astype(BF16)


def _layer_norm(x, g, b):
    mu = jnp.mean(x, axis=-1, keepdims=True)
    xc = x - mu
    var = jnp.mean(xc * xc, axis=-1, keepdims=True)
    return xc * lax.rsqrt(var + LN_EPS) * g + b


def _head_query(qt, hh, head, nh, dh):
    lane = lax.broadcasted_iota(jnp.int32, qt.shape, 1)
    qm = jnp.where((lane >= hh * dh) & (lane < (hh + 1) * dh), qt, jnp.zeros_like(qt))
    ones = jnp.where((lane == head) | (lane == head + nh) | (lane == head + 2 * nh), 1.0, 0.0)
    return jnp.concatenate([qm, ones.astype(BF16)], axis=1)


def _proj_kernel(*refs, s, tm, hdr, nh, aliased):
    (x_ref, st_ref, wq_ref, wk_ref, wv_ref, wf_ref, wb_ref, wc_ref, wh_ref,
     bf_ref, cw_ref, cb_ref) = refs[:12]
    rest = refs[14:] if aliased else refs[12:]
    (q_ref, k_ref, v_ref, logf_ref, caug_ref, cf_ref, conv_ref, ust_ref, ubuf, ccar) = rest
    t = pl.program_id(1)
    xb = x_ref[0].astype(BF16)

    q_ref[0] = _dot(xb, wq_ref[...]).astype(BF16)
    k_ref[0, 0] = _dot(xb, wk_ref[...])
    v_ref[0, 0] = _dot(xb, wv_ref[...])

    fl = _dot(xb, wf_ref[...]) + bf_ref[...]
    logf = jnp.minimum(fl, 0.0) - jnp.log1p(jnp.exp(-jnp.abs(fl)))
    logf_ref[0] = logf[:, :nh]

    @pl.when(t == 0)
    def _():
        ccar[...] = jnp.zeros_like(ccar)

    cs = _stride_cumsum(logf, s)
    if s == 1:
        c = cs + ccar[0:1, :]
        ccar[...] = jnp.broadcast_to(c[tm - 1:tm, :], ccar.shape)
    else:
        c = cs + jnp.concatenate([ccar[...]] * (tm // s), axis=0)
        ccar[...] = c[tm - s:tm, :]
    cf_ref[0] = c
    caug_ref[0] = _split3(-c, nh)

    gb = _dot(xb, wb_ref[...])
    u = _dot(xb, wc_ref[...]) * _dot(xb, wh_ref[...])

    @pl.when(t == 0)
    def _():
        ubuf[hdr - 2 * s:hdr, :] = st_ref[0]

    ubuf[hdr:hdr + tm, :] = u
    u1 = ubuf[hdr - s:hdr - s + tm, :]
    u2 = ubuf[hdr - 2 * s:hdr - 2 * s + tm, :]
    cw = cw_ref[...]
    conv = cb_ref[...] + cw[0:1, :] * u2
    conv = conv + cw[1:2, :] * u1
    conv = conv + cw[2:3, :] * u
    conv_ref[0] = (gb * conv).astype(BF16)
    tail = ubuf[hdr + tm - 2 * s:hdr + tm, :]
    ust_ref[0] = tail
    ubuf[hdr - 2 * s:hdr, :] = tail


def _proj(x, st, w, l, s, kv_bufs):
    ns, L, D = x.shape
    depth = w["wq"].shape[0]
    da = w["wq"].shape[2]
    dc = w["wb"].shape[2]
    nh = w["nh"]
    tm = min(ROW_TILE, L)
    nt = L // tm
    assert L % tm == 0 and (nt == 1 or s == 1 or tm % s == 0)
    hdr = max(SUBLANES, 2 * s)
    aliased = kv_bufs is not None

    def wspec(shape):
        return pl.BlockSpec((None,) + shape, lambda n, t: (l, 0, 0), pipeline_mode=pl.Buffered(1))

    in_specs = [
        pl.BlockSpec((1, tm, D), lambda n, t: (n, t, 0)),
        pl.BlockSpec((1, 2 * s, dc), lambda n, t: (n, 0, 0)),
        wspec((D, da)), wspec((D, da)), wspec((D, da)), wspec((D, LANES)),
        wspec((D, dc)), wspec((D, dc)), wspec((D, dc)),
        wspec((1, LANES)), wspec((3, dc)), wspec((1, dc)),
    ]
    args = [x, st, w["wq"], w["wk"], w["wv"], w["wf"], w["wb"], w["wc"], w["wh"],
            w["bf"], w["cw"], w["cb"]]
    io_alias = {}
    if aliased:
        in_specs += [pl.BlockSpec(memory_space=pl.ANY), pl.BlockSpec(memory_space=pl.ANY)]
        args += list(kv_bufs)
        io_alias = {12: 1, 13: 2}
    kv_shape = jax.ShapeDtypeStruct((depth, ns, L, da), F32)
    kv_spec = pl.BlockSpec((1, 1, tm, da), lambda n, t: (l, n, t, 0))
    out_shape = [
        jax.ShapeDtypeStruct((ns, L, da), BF16), kv_shape, kv_shape,
        jax.ShapeDtypeStruct((ns, L, nh), F32),
        jax.ShapeDtypeStruct((ns, L, LANES), BF16),
        jax.ShapeDtypeStruct((ns, L, LANES), F32),
        jax.ShapeDtypeStruct((ns, L, dc), BF16),
        jax.ShapeDtypeStruct((ns, 2 * s, dc), F32),
    ]
    out_specs = [
        pl.BlockSpec((1, tm, da), lambda n, t: (n, t, 0)), kv_spec, kv_spec,
        pl.BlockSpec((1, tm, nh), lambda n, t: (n, t, 0)),
        pl.BlockSpec((1, tm, LANES), lambda n, t: (n, t, 0)),
        pl.BlockSpec((1, tm, LANES), lambda n, t: (n, t, 0)),
        pl.BlockSpec((1, tm, dc), lambda n, t: (n, t, 0)),
        pl.BlockSpec((1, 2 * s, dc), lambda n, t: (n, 0, 0)),
    ]
    return pl.pallas_call(
        functools.partial(_proj_kernel, s=s, tm=tm, hdr=hdr, nh=nh, aliased=aliased),
        grid=(ns, nt),
        in_specs=in_specs, out_specs=out_specs, out_shape=out_shape,
        scratch_shapes=[pltpu.VMEM((hdr + tm, dc), F32), pltpu.VMEM((max(s, SUBLANES), LANES), F32)],
        input_output_aliases=io_alias,
        compiler_params=pltpu.CompilerParams(
            dimension_semantics=("arbitrary", "arbitrary"), vmem_limit_bytes=VMEM_LIMIT),
        name="proj",
    )(*args)


def _attn_prompt_kernel(q_ref, k_ref, v_ref, caug_ref, o_ref, kaug, vt, *, T, tq, nh, dh):
    p = pl.program_id(1)
    kaug[:, 0:LANES] = k_ref[0, 0].astype(BF16)
    kaug[:, LANES:2 * LANES] = caug_ref[0]
    vt[...] = v_ref[0, 0].T.astype(BF16)
    row = lax.broadcasted_iota(jnp.int32, (tq, tq), 0)
    col = lax.broadcasted_iota(jnp.int32, (tq, tq), 1)
    causal = row <= col
    for i in range(T // tq):
        l0 = i * tq
        qt = q_ref[0, l0:l0 + tq, :]
        outs = []
        for hh in range(2):
            qa = _head_query(qt, hh, 2 * p + hh, nh, dh)
            vh = vt.at[hh * dh:(hh + 1) * dh, :]
            sd = jnp.where(causal, _dot_nt(kaug[l0:l0 + tq, :], qa), -jnp.inf)
            m = jnp.max(sd, axis=0, keepdims=True)
            if i > 0:
                so = _dot_nt(kaug[0:l0, :], qa)
                m = jnp.maximum(m, jnp.max(so, axis=0, keepdims=True))
            pd = jnp.exp(sd - m)
            lsum = jnp.sum(pd, axis=0, keepdims=True)
            ot = _dot(vh[:, l0:l0 + tq], pd.astype(BF16))
            if i > 0:
                po = jnp.exp(so - m)
                lsum = lsum + jnp.sum(po, axis=0, keepdims=True)
                ot = ot + _dot(vh[:, 0:l0], po.astype(BF16))
            outs.append(ot / lsum)
        o_ref[0, l0:l0 + tq, :] = jnp.concatenate(outs, axis=0).T.astype(BF16)


def _attn_prompt(q, k_all, v_all, caug, l, nh, dh):
    b, T, da = q.shape
    tq = min(Q_TILE, T)
    npair = da // LANES
    return pl.pallas_call(
        functools.partial(_attn_prompt_kernel, T=T, tq=tq, nh=nh, dh=dh),
        grid=(b, npair),
        in_specs=[
            pl.BlockSpec((1, T, LANES), lambda n, p: (n, 0, p)),
            pl.BlockSpec((1, 1, T, LANES), lambda n, p: (l, n, 0, p)),
            pl.BlockSpec((1, 1, T, LANES), lambda n, p: (l, n, 0, p)),
            pl.BlockSpec((1, T, LANES), lambda n, p: (n, 0, 0)),
        ],
        out_specs=pl.BlockSpec((1, T, LANES), lambda n, p: (n, 0, p)),
        out_shape=jax.ShapeDtypeStruct((b, T, da), BF16),
        scratch_shapes=[pltpu.VMEM((T, 2 * LANES), BF16), pltpu.VMEM((LANES, T), BF16)],
        compiler_params=pltpu.CompilerParams(
            dimension_semantics=("arbitrary", "arbitrary"), vmem_limit_bytes=VMEM_LIMIT),
        name="attn_prompt",
    )(q, k_all, v_all, caug)


def _attn_sample_kernel(q_ref, kn_ref, vn_ref, cf_ref, ck_ref, cv_ref, clf_ref, o_ref,
                        kaug_c, vt_c, kaug_n, vn_pad, cscr, qa_scr, *, P, dt, nh, dh):
    p = pl.program_id(1)
    cscr[...] = jnp.zeros_like(cscr)
    cscr[:, 0:3 * nh] = clf_ref[0]
    cc = _stride_cumsum(cscr[...], 1)
    kaug_c[:, 0:LANES] = ck_ref[0].astype(BF16)
    kaug_c[:, LANES:2 * LANES] = _split3(-cc, nh)
    vt_c[...] = cv_ref[0].T.astype(BF16)
    cn = cc[P - 1:P, :] + cf_ref[0]
    kaug_n[...] = jnp.zeros_like(kaug_n)
    kaug_n[0:dt, 0:LANES] = kn_ref[0].astype(BF16)
    kaug_n[0:dt, LANES:2 * LANES] = _split3(-cn, nh)
    vn_pad[...] = jnp.zeros_like(vn_pad)
    vn_pad[0:dt, :] = vn_ref[0]
    vt_n = vn_pad[...].T.astype(BF16)
    qt = q_ref[0]
    qa_scr[...] = jnp.zeros_like(qa_scr)
    qa_scr[0:dt, :] = _head_query(qt, 0, 2 * p, nh, dh)
    qa_scr[dt:2 * dt, :] = _head_query(qt, 1, 2 * p + 1, nh, dh)
    qa = qa_scr[...]

    sc = _dot_nt(kaug_c[...], qa)
    sn = _dot_nt(kaug_n[...], qa)
    row = lax.broadcasted_iota(jnp.int32, sn.shape, 0)
    col = lax.broadcasted_iota(jnp.int32, sn.shape, 1)
    tq = jnp.where(col < dt, col, col - dt)
    sn = jnp.where((row < dt) & (row <= tq), sn, -jnp.inf)
    m = jnp.maximum(jnp.max(sc, axis=0, keepdims=True), jnp.max(sn, axis=0, keepdims=True))
    pc = jnp.exp(sc - m)
    pn = jnp.exp(sn - m)
    lsum = jnp.sum(pc, axis=0, keepdims=True) + jnp.sum(pn, axis=0, keepdims=True)
    ot = _dot(vt_c[...], pc.astype(BF16)) + _dot(vt_n, pn.astype(BF16))
    o = (ot / lsum).T
    lane = lax.broadcasted_iota(jnp.int32, (dt, LANES), 1)
    o_ref[0] = jnp.where(lane < dh, o[0:dt, :], o[dt:2 * dt, :]).astype(BF16)


def _attn_sample(q, k_all, v_all, cf, cache_k, cache_v, cache_lf3, l, nh, dh):
    db, dt, da = q.shape
    P = cache_k.shape[2]
    npair = da // LANES
    assert 2 * dt <= LANES and dt % 16 == 0
    return pl.pallas_call(
        functools.partial(_attn_sample_kernel, P=P, dt=dt, nh=nh, dh=dh),
        grid=(db, npair),
        in_specs=[
            pl.BlockSpec((1, dt, LANES), lambda n, p: (n, 0, p)),
            pl.BlockSpec((1, dt, LANES), lambda n, p: (n, 0, p)),
            pl.BlockSpec((1, dt, LANES), lambda n, p: (n, 0, p)),
            pl.BlockSpec((1, dt, LANES), lambda n, p: (n, 0, 0)),
            pl.BlockSpec((None, 1, P, LANES), lambda n, p: (l, n, 0, p)),
            pl.BlockSpec((None, 1, P, LANES), lambda n, p: (l, n, 0, p)),
            pl.BlockSpec((None, 1, P, 3 * nh), lambda n, p: (l, n, 0, 0)),
        ],
        out_specs=pl.BlockSpec((1, dt, LANES), lambda n, p: (n, 0, p)),
        out_shape=jax.ShapeDtypeStruct((db, dt, da), BF16),
        scratch_shapes=[
            pltpu.VMEM((P, 2 * LANES), BF16), pltpu.VMEM((LANES, P), BF16),
            pltpu.VMEM((LANES, 2 * LANES), BF16), pltpu.VMEM((LANES, LANES), F32),
            pltpu.VMEM((P, LANES), F32), pltpu.VMEM((LANES, 2 * LANES), BF16),
        ],
        compiler_params=pltpu.CompilerParams(
            dimension_semantics=("arbitrary", "arbitrary"), vmem_limit_bytes=VMEM_LIMIT),
        name="attn_sample",
    )(q, k_all, v_all, cf, cache_k, cache_v, cache_lf3)


def _ffn_kernel(x_ref, at_ref, cv_ref, gst_ref, woa_ref, woc_ref, g1_ref, b1_ref,
                wug_ref, wuv_ref, fw_ref, fb_ref, wdn_ref, g2_ref, b2_ref,
                y_ref, gout_ref, gbuf, a_scr, *, s, tm, hdr, fc, alpha):
    t = pl.program_id(1)
    mixed = _dot(at_ref[0], woa_ref[...]) + _dot(cv_ref[0], woc_ref[...])
    x1 = _layer_norm(alpha * x_ref[0] + mixed, g1_ref[...], b1_ref[...])
    x1b = x1.astype(BF16)

    @pl.when(t == 0)
    def _():
        gbuf[hdr - 2 * s:hdr, :] = gst_ref[0]

    fw = fw_ref[...]
    fb = fb_ref[...]
    for c in range(wug_ref.shape[0]):
        cs = slice(c * fc, (c + 1) * fc)
        g = _dot(x1b, wug_ref[c])
        val = _dot(x1b, wuv_ref[c])
        gbuf[hdr:hdr + tm, cs] = g
        g1 = gbuf[hdr - s:hdr - s + tm, cs]
        g2 = gbuf[hdr - 2 * s:hdr - 2 * s + tm, cs]
        y = fb[:, cs] + fw[0:1, cs] * g2
        y = y + fw[1:2, cs] * g1
        y = y + fw[2:3, cs] * g
        a = y * (1.0 / (1.0 + jnp.exp(-y))) * val
        a_scr[:, cs] = a.astype(BF16)
    down = _dot(a_scr[...], wdn_ref[...])
    y_ref[0] = _layer_norm(alpha * x1 + down, g2_ref[...], b2_ref[...])
    tail = gbuf[hdr + tm - 2 * s:hdr + tm, :]
    gout_ref[0] = tail
    gbuf[hdr - 2 * s:hdr, :] = tail


def _ffn(x, attn, conv, gst, w, l, s, alpha):
    ns, L, D = x.shape
    da = attn.shape[2]
    dc = conv.shape[2]
    nch, _, fc = w["wug"].shape[1:]
    dff = nch * fc
    tm = min(ROW_TILE, L)
    nt = L // tm
    hdr = max(SUBLANES, 2 * s)

    def wspec(shape):
        idx = (l,) + (0,) * len(shape)
        return pl.BlockSpec((None,) + shape, lambda n, t: idx, pipeline_mode=pl.Buffered(1))

    row = lambda width: pl.BlockSpec((1, tm, width), lambda n, t: (n, t, 0))
    return pl.pallas_call(
        functools.partial(_ffn_kernel, s=s, tm=tm, hdr=hdr, fc=fc, alpha=alpha),
        grid=(ns, nt),
        in_specs=[
            row(D), row(da), row(dc),
            pl.BlockSpec((1, 2 * s, dff), lambda n, t: (n, 0, 0)),
            wspec((da, D)), wspec((dc, D)), wspec((1, D)), wspec((1, D)),
            wspec((nch, D, fc)), wspec((nch, D, fc)), wspec((3, dff)), wspec((1, dff)),
            wspec((dff, D)), wspec((1, D)), wspec((1, D)),
        ],
        out_specs=[row(D), pl.BlockSpec((1, 2 * s, dff), lambda n, t: (n, 0, 0))],
        out_shape=[jax.ShapeDtypeStruct((ns, L, D), F32), jax.ShapeDtypeStruct((ns, 2 * s, dff), F32)],
        scratch_shapes=[pltpu.VMEM((hdr + tm, dff), F32), pltpu.VMEM((tm, dff), BF16)],
        compiler_params=pltpu.CompilerParams(
            dimension_semantics=("arbitrary", "arbitrary"), vmem_limit_bytes=VMEM_LIMIT),
        name="ffn",
    )(x, attn, conv, gst, w["woa"], w["woc"], w["g1"], w["b1"], w["wug"], w["wuv"],
      w["fw"], w["fb"], w["wdn"], w["g2"], w["b2"])


def _prep_weights(w_in, b_f, conv_w, conv_b, w_out, ln1_g, ln1_b, w_up, ffn_conv_w, ffn_conv_b,
                  w_down, ln2_g, ln2_b, dh):
    depth, D, _ = w_in.shape
    nh = b_f.shape[1]
    da = nh * dh
    dc = conv_w.shape[2]
    dff = ffn_conv_w.shape[2]
    assert dh == 64 and da % LANES == 0 and 3 * nh <= LANES and dff % FF_CHUNK == 0
    o = 3 * da + nh
    pad = LANES - 3 * nh
    wf = jnp.pad(jnp.tile(w_in[:, :, 3 * da:o], (1, 1, 3)), ((0, 0), (0, 0), (0, pad)))
    bf = jnp.pad(jnp.tile(b_f, (1, 3)), ((0, 0), (0, pad)))[:, None, :]
    nch = dff // FF_CHUNK

    def chunks(m):
        return m.reshape(depth, D, nch, FF_CHUNK).transpose(0, 2, 1, 3).astype(BF16)

    return dict(
        nh=nh,
        wq=(w_in[:, :, 0:da] * (dh ** -0.5)).astype(BF16),
        wk=w_in[:, :, da:2 * da].astype(BF16),
        wv=w_in[:, :, 2 * da:3 * da].astype(BF16),
        wf=wf.astype(BF16), bf=bf,
        wb=w_in[:, :, o:o + dc].astype(BF16),
        wc=w_in[:, :, o + dc:o + 2 * dc].astype(BF16),
        wh=w_in[:, :, o + 2 * dc:o + 3 * dc].astype(BF16),
        cw=conv_w, cb=conv_b[:, None, :],
        woa=w_out[:, 0:da].astype(BF16), woc=w_out[:, da:].astype(BF16),
        g1=ln1_g[:, None, :], b1=ln1_b[:, None, :],
        wug=chunks(w_up[:, :, 0:dff]), wuv=chunks(w_up[:, :, dff:]),
        fw=ffn_conv_w, fb=ffn_conv_b[:, None, :],
        wdn=w_down.astype(BF16), g2=ln2_g[:, None, :], b2=ln2_b[:, None, :],
    )


def kernel(x_prompt, x_sample, cache_k, cache_v, cache_logf, state_mix_conv, state_ffn_conv, w_in, b_f, conv_w, conv_b, w_out, ln1_g, ln1_b, w_up, ffn_conv_w, ffn_conv_b, w_down, ln2_g, ln2_b):
    depth = w_in.shape[0]
    B, T, D = x_prompt.shape
    DB, DT, _ = x_sample.shape
    P, nh, dh = cache_k.shape[2:]
    da = nh * dh
    dc = conv_w.shape[2]
    dff = ffn_conv_w.shape[2]
    alpha = (2 * depth) ** 0.25
    assert DB % SUBLANES == 0
    w = _prep_weights(w_in, b_f, conv_w, conv_b, w_out, ln1_g, ln1_b, w_up, ffn_conv_w, ffn_conv_b,
                      w_down, ln2_g, ln2_b, dh)

    ck = cache_k.reshape(depth, DB, P, da)
    cv = cache_v.reshape(depth, DB, P, da)
    clf3 = jnp.tile(cache_logf, (1, 1, 1, 3))

    def time_major(a):
        return a.transpose(0, 2, 1, 3).reshape(depth, 1, a.shape[2] * DB, a.shape[3])

    mix_tm = time_major(state_mix_conv)
    ffn_tm = time_major(state_ffn_conv)
    zero_mix = jnp.zeros((B, 2, dc), F32)
    zero_ffn = jnp.zeros((B, 2, dff), F32)

    yp = x_prompt
    ys = x_sample.transpose(1, 0, 2).reshape(1, DT * DB, D)
    kvp = kvs = None
    logf_p, mix_p, ffn_p, logf_s, mix_s, ffn_s = [], [], [], [], [], []
    for l in range(depth):
        q, kp, vp, lf, caug, _, conv, ust = _proj(yp, zero_mix, w, l, 1, kvp)
        kvp = (kp, vp)
        attn = _attn_prompt(q, kp, vp, caug, l, nh, dh)
        yp, gst = _ffn(yp, attn, conv, zero_ffn, w, l, 1, alpha)
        logf_p.append(lf); mix_p.append(ust); ffn_p.append(gst)

        q, ks, vs, lf, _, cf, conv, ust = _proj(ys, mix_tm[l], w, l, DB, kvs)
        kvs = (ks, vs)

        def batch_major(a):
            return a.reshape(-1, DB, a.shape[-1]).transpose(1, 0, 2)

        attn = _attn_sample(batch_major(q), batch_major(ks[l]), batch_major(vs[l]), batch_major(cf),
                            ck, cv, clf3, l, nh, dh)
        attn = attn.transpose(1, 0, 2).reshape(1, DT * DB, da)
        ys, gst = _ffn(ys, attn, conv, ffn_tm[l], w, l, DB, alpha)
        logf_s.append(batch_major(lf)); mix_s.append(batch_major(ust)); ffn_s.append(batch_major(gst))

    k_s = kvs[0].reshape(depth, DT, DB, nh, dh).transpose(0, 2, 1, 3, 4)
    v_s = kvs[1].reshape(depth, DT, DB, nh, dh).transpose(0, 2, 1, 3, 4)
    return (yp, ys.reshape(DT, DB, D).transpose(1, 0, 2),
            kvp[0].reshape(depth, B, T, nh, dh), kvp[1].reshape(depth, B, T, nh, dh),
            jnp.stack(logf_p), jnp.stack(mix_p), jnp.stack(ffn_p),
            k_s, v_s, jnp.stack(logf_s), jnp.stack(mix_s), jnp.stack(ffn_s))
```

```python
import functools

import jax
import jax.numpy as jnp
from jax import lax
from jax.experimental import pallas as pl
from jax.experimental.pallas import tpu as pltpu

F32 = jnp.float32
BF16 = jnp.bfloat16

LANES = 128
SUBLANES = 8
LN_EPS = 1e-5
ROW_TILE = 512
ROW_SPLITS = 2
Q_TILE = 256
FF_CHUNK = 256
QK_AHEAD = 4
KV_BLOCKS = 1
VMEM_LIMIT = 56 * 1024 * 1024


def _dot(a, b):
    return jnp.dot(a, b, preferred_element_type=F32)


def _dot_nt(a, b):
    return lax.dot_general(a, b, (((1,), (1,)), ((), ())), preferred_element_type=F32)


def _stride_cumsum(x, s):
    rows = x.shape[0]
    row = lax.broadcasted_iota(jnp.int32, x.shape, 0)
    d = s
    while d < rows:
        x = x + jnp.where(row >= d, pltpu.roll(x, d, axis=0), 0.0)
        d *= 2
    return x


def _split3(v, nh):
    hi = v.astype(BF16).astype(F32)
    r1 = v - hi
    mid = r1.astype(BF16).astype(F32)
    lo = (r1 - mid).astype(BF16).astype(F32)
    lane = lax.broadcasted_iota(jnp.int32, v.shape, 1)
    out = jnp.where(lane < nh, hi, jnp.where(lane < 2 * nh, mid, jnp.where(lane < 3 * nh, lo, 0.0)))
    return out.astype(BF16)


def _layer_norm(x, g, b):
    mu = jnp.mean(x, axis=-1, keepdims=True)
    xc = x - mu
    var = jnp.mean(xc * xc, axis=-1, keepdims=True)
    return xc * lax.rsqrt(var + LN_EPS) * g + b


def _head_query(qt, hh, head, nh, dh):
    lane = lax.broadcasted_iota(jnp.int32, qt.shape, 1)
    qm = jnp.where((lane >= hh * dh) & (lane < (hh + 1) * dh), qt, jnp.zeros_like(qt))
    ones = jnp.where((lane == head) | (lane == head + nh) | (lane == head + 2 * nh), 1.0, 0.0)
    return jnp.concatenate([qm, ones.astype(BF16)], axis=1)


def _proj_kernel(*refs, s, tm, hdr, nh, aliased):
    (x_ref, st_ref, wq_ref, wk_ref, wv_ref, wf_ref, wb_ref, wc_ref, wh_ref,
     bf_ref, cw_ref, cb_ref) = refs[:12]
    rest = refs[14:] if aliased else refs[12:]
    (q_ref, k_ref, v_ref, logf_ref, caug_ref, cf_ref, conv_ref, ust_ref, ubuf, ccar) = rest
    t = pl.program_id(1)

    @pl.when(t == 0)
    def _():
        ccar[...] = jnp.zeros_like(ccar)
        ubuf[hdr - 2 * s:hdr, :] = st_ref[0]

    cw = cw_ref[...]
    rs = tm // ROW_SPLITS
    for h in range(ROW_SPLITS):
        r = slice(h * rs, (h + 1) * rs)
        xb = x_ref[0, r, :].astype(BF16)
        q_ref[0, r, :] = _dot(xb, wq_ref[...]).astype(BF16)
        k_ref[0, 0, r, :] = _dot(xb, wk_ref[...])
        v_ref[0, 0, r, :] = _dot(xb, wv_ref[...])

        fl = _dot(xb, wf_ref[...]) + bf_ref[...]
        logf = jnp.minimum(fl, 0.0) - jnp.log1p(jnp.exp(-jnp.abs(fl)))
        logf_ref[0, r, :] = logf[:, :nh]
        cs = _stride_cumsum(logf, s)
        if s == 1:
            c = cs + ccar[0:1, :]
            ccar[...] = jnp.broadcast_to(c[rs - 1:rs, :], ccar.shape)
        else:
            c = cs + jnp.concatenate([ccar[...]] * (rs // s), axis=0)
            ccar[...] = c[rs - s:rs, :]
        cf_ref[0, r, :] = c
        caug_ref[0, r, :] = _split3(-c, nh)

        gb = _dot(xb, wb_ref[...])
        u = _dot(xb, wc_ref[...]) * _dot(xb, wh_ref[...])
        lo = hdr + h * rs
        ubuf[lo:lo + rs, :] = u
        u1 = ubuf[lo - s:lo - s + rs, :]
        u2 = ubuf[lo - 2 * s:lo - 2 * s + rs, :]
        conv = cb_ref[...] + cw[0:1, :] * u2
        conv = conv + cw[1:2, :] * u1
        conv = conv + cw[2:3, :] * u
        conv_ref[0, r, :] = (gb * conv).astype(BF16)
    tail = ubuf[hdr + tm - 2 * s:hdr + tm, :]
    ust_ref[0] = tail
    ubuf[hdr - 2 * s:hdr, :] = tail


def _proj(x, st, w, l, s, kv_bufs):
    ns, L, D = x.shape
    depth = w["wq"].shape[0]
    da = w["wq"].shape[2]
    dc = w["wb"].shape[2]
    nh = w["nh"]
    tm = min(ROW_TILE, L)
    nt = L // tm
    assert L % tm == 0 and tm % ROW_SPLITS == 0 and (tm // ROW_SPLITS) % max(s, SUBLANES) == 0
    hdr = max(SUBLANES, 2 * s)
    aliased = kv_bufs is not None

    def wspec(shape):
        return pl.BlockSpec((None,) + shape, lambda n, t: (l, 0, 0), pipeline_mode=pl.Buffered(1))

    in_specs = [
        pl.BlockSpec((1, tm, D), lambda n, t: (n, t, 0)),
        pl.BlockSpec((1, 2 * s, dc), lambda n, t: (n, 0, 0)),
        wspec((D, da)), wspec((D, da)), wspec((D, da)), wspec((D, LANES)),
        wspec((D, dc)), wspec((D, dc)), wspec((D, dc)),
        wspec((1, LANES)), wspec((3, dc)), wspec((1, dc)),
    ]
    args = [x, st, w["wq"], w["wk"], w["wv"], w["wf"], w["wb"], w["wc"], w["wh"],
            w["bf"], w["cw"], w["cb"]]
    io_alias = {}
    if aliased:
        in_specs += [pl.BlockSpec(memory_space=pl.ANY), pl.BlockSpec(memory_space=pl.ANY)]
        args += list(kv_bufs)
        io_alias = {12: 1, 13: 2}
    kv_shape = jax.ShapeDtypeStruct((depth, ns, L, da), F32)
    kv_spec = pl.BlockSpec((1, 1, tm, da), lambda n, t: (l, n, t, 0))
    out_shape = [
        jax.ShapeDtypeStruct((ns, L, da), BF16), kv_shape, kv_shape,
        jax.ShapeDtypeStruct((ns, L, nh), F32),
        jax.ShapeDtypeStruct((ns, L, LANES), BF16),
        jax.ShapeDtypeStruct((ns, L, LANES), F32),
        jax.ShapeDtypeStruct((ns, L, dc), BF16),
        jax.ShapeDtypeStruct((ns, 2 * s, dc), F32),
    ]
    out_specs = [
        pl.BlockSpec((1, tm, da), lambda n, t: (n, t, 0)), kv_spec, kv_spec,
        pl.BlockSpec((1, tm, nh), lambda n, t: (n, t, 0)),
        pl.BlockSpec((1, tm, LANES), lambda n, t: (n, t, 0)),
        pl.BlockSpec((1, tm, LANES), lambda n, t: (n, t, 0)),
        pl.BlockSpec((1, tm, dc), lambda n, t: (n, t, 0)),
        pl.BlockSpec((1, 2 * s, dc), lambda n, t: (n, 0, 0)),
    ]
    return pl.pallas_call(
        functools.partial(_proj_kernel, s=s, tm=tm, hdr=hdr, nh=nh, aliased=aliased),
        grid=(ns, nt),
        in_specs=in_specs, out_specs=out_specs, out_shape=out_shape,
        scratch_shapes=[pltpu.VMEM((hdr + tm, dc), F32), pltpu.VMEM((max(s, SUBLANES), LANES), F32)],
        input_output_aliases=io_alias,
        compiler_params=pltpu.CompilerParams(
            dimension_semantics=("arbitrary", "arbitrary"), vmem_limit_bytes=VMEM_LIMIT),
        name="proj",
    )(*args)


def _attn_prompt_kernel(q_ref, k_ref, v_ref, caug_ref, o_ref, kaug, vt, qa_scr, *, T, tq, nh, dh):
    p = pl.program_id(1)
    kaug[:, 0:LANES] = k_ref[0, 0].astype(BF16)
    kaug[:, LANES:2 * LANES] = caug_ref[0]
    v_t = v_ref[0, 0].T.astype(BF16)
    for hh in range(2):
        vt[hh, 0:dh, :] = v_t[hh * dh:(hh + 1) * dh, :]
        vt[hh, dh:, :] = jnp.ones((vt.shape[1] - dh, T), BF16)
    row = lax.broadcasted_iota(jnp.int32, (tq, tq), 0)
    col = lax.broadcasted_iota(jnp.int32, (tq, tq), 1)
    causal = row <= col
    nq = T // tq
    for c in range(nq):
        qt = q_ref[0, c * tq:(c + 1) * tq, :]
        for hh in range(2):
            qa_scr[c, hh] = _head_query(qt, hh, 2 * p + hh, nh, dh)
    streams = [[(c, hh, r0, min(r0 + KV_BLOCKS, c + 1)) for r0 in range(0, c + 1, KV_BLOCKS)]
               for c in range(nq) for hh in range(2)]
    units = [st[r] for r in range(max(map(len, streams))) for st in streams if r < len(st)]

    def scores(u):
        c, hh, r0, r1 = u
        return _dot_nt(kaug[r0 * tq:r1 * tq, :], qa_scr[c, hh])

    pending = {k: scores(units[k]) for k in range(min(QK_AHEAD, len(units)))}
    state, outs = {}, {}
    for k, (c, hh, r0, r1) in enumerate(units):
        s = pending.pop(k)
        if k + QK_AHEAD < len(units):
            pending[k + QK_AHEAD] = scores(units[k + QK_AHEAD])
        last = r1 == c + 1
        if last:
            sd = jnp.where(causal, s[-tq:, :], -jnp.inf)
            s = sd if r1 - r0 == 1 else jnp.concatenate([s[:-tq, :], sd], axis=0)
        vj = vt[hh, :, r0 * tq:r1 * tq]
        mc = jnp.max(s, axis=0, keepdims=True)
        if r0 == 0:
            m = mc
            acc = _dot(vj, jnp.exp(s - m).astype(BF16))
        else:
            m0, a0 = state.pop((c, hh))
            m = jnp.maximum(m0, mc)
            acc = jnp.exp(m0 - m) * a0 + _dot(vj, jnp.exp(s - m).astype(BF16))
        if last:
            outs[hh] = acc[0:dh, :] / acc[dh:dh + 1, :]
            if hh == 1:
                o_ref[0, c * tq:(c + 1) * tq, :] = jnp.concatenate(
                    [outs[0], outs[1]], axis=0).T.astype(BF16)
        else:
            state[c, hh] = (m, acc)


def _attn_prompt(q, k_all, v_all, caug, l, nh, dh):
    b, T, da = q.shape
    tq = min(Q_TILE, T)
    npair = da // LANES
    return pl.pallas_call(
        functools.partial(_attn_prompt_kernel, T=T, tq=tq, nh=nh, dh=dh),
        grid=(b, npair),
        in_specs=[
            pl.BlockSpec((1, T, LANES), lambda n, p: (n, 0, p)),
            pl.BlockSpec((1, 1, T, LANES), lambda n, p: (l, n, 0, p)),
            pl.BlockSpec((1, 1, T, LANES), lambda n, p: (l, n, 0, p)),
            pl.BlockSpec((1, T, LANES), lambda n, p: (n, 0, 0)),
        ],
        out_specs=pl.BlockSpec((1, T, LANES), lambda n, p: (n, 0, p)),
        out_shape=jax.ShapeDtypeStruct((b, T, da), BF16),
        scratch_shapes=[pltpu.VMEM((T, 2 * LANES), BF16), pltpu.VMEM((2, dh + 16, T), BF16),
                        pltpu.VMEM((T // tq, 2, tq, 2 * LANES), BF16)],
        compiler_params=pltpu.CompilerParams(
            dimension_semantics=("arbitrary", "arbitrary"), vmem_limit_bytes=VMEM_LIMIT),
        name="attn_prompt",
    )(q, k_all, v_all, caug)


def _lane_cumsum(x):
    n = x.shape[1]
    lane = lax.broadcasted_iota(jnp.int32, x.shape, 1)
    d = 1
    while d < n:
        x = x + jnp.where(lane >= d, pltpu.roll(x, d, axis=1), 0.0)
        d *= 2
    return x


def _attn_sample_kernel(q_ref, kn_ref, vn_ref, cft_ref, ck_ref, cv_ref, clf_ref, o_ref,
                        *, P, dt, nh, dh):
    cc = _lane_cumsum(clf_ref[...])
    cn = cc[:, P - 1:P] + cft_ref[0]
    lane = lax.broadcasted_iota(jnp.int32, (dt, LANES), 1)
    row = lax.broadcasted_iota(jnp.int32, (2 * dt, dt), 0)
    col = lax.broadcasted_iota(jnp.int32, (2 * dt, dt), 1)
    causal = col <= jnp.where(row < dt, row, row - dt)
    for p in range(ck_ref.shape[0] // LANES):
        pr = slice(p * LANES, (p + 1) * LANES)
        kt = ck_ref[pr, :].astype(BF16)
        vt = cv_ref[pr, :].astype(BF16)
        qt = q_ref[0, :, pr]
        q2 = jnp.concatenate([jnp.where(lane < dh, qt, jnp.zeros_like(qt)),
                              jnp.where(lane >= dh, qt, jnp.zeros_like(qt))], axis=0)

        def per_head(c, width):
            return jnp.concatenate([jnp.broadcast_to(c[2 * p:2 * p + 1, :], (dt, width)),
                                    jnp.broadcast_to(c[2 * p + 1:2 * p + 2, :], (dt, width))], axis=0)

        sc = _dot(q2, kt) - per_head(cc, P)
        sn = _dot_nt(q2, kn_ref[0, :, pr].astype(BF16)) - per_head(cn, dt)
        sn = jnp.where(causal, sn, -jnp.inf)
        m = jnp.maximum(jnp.max(sc, axis=1, keepdims=True), jnp.max(sn, axis=1, keepdims=True))
        pc = jnp.exp(sc - m)
        pn = jnp.exp(sn - m)
        lsum = jnp.sum(pc, axis=1, keepdims=True) + jnp.sum(pn, axis=1, keepdims=True)
        o = _dot_nt(pc.astype(BF16), vt) + _dot(pn.astype(BF16), vn_ref[0, :, pr].astype(BF16))
        o = o / lsum
        o_ref[0, :, pr] = jnp.where(lane < dh, o[0:dt, :], o[dt:2 * dt, :]).astype(BF16)


def _attn_sample(q, k_new, v_new, cft, cache_kt, cache_vt, cache_lft, l, nh, dh):
    db, dt, da = q.shape
    P = cache_kt.shape[3]
    assert dt % 16 == 0
    tok = lambda width: pl.BlockSpec((1, dt, width), lambda n: (n, 0, 0))
    return pl.pallas_call(
        functools.partial(_attn_sample_kernel, P=P, dt=dt, nh=nh, dh=dh),
        grid=(db,),
        in_specs=[
            tok(da), tok(da), tok(da),
            pl.BlockSpec((1, nh, dt), lambda n: (n, 0, 0)),
            pl.BlockSpec((None, None, da, P), lambda n: (l, n, 0, 0)),
            pl.BlockSpec((None, None, da, P), lambda n: (l, n, 0, 0)),
            pl.BlockSpec((None, None, nh, P), lambda n: (l, n, 0, 0)),
        ],
        out_specs=tok(da),
        out_shape=jax.ShapeDtypeStruct((db, dt, da), BF16),
        compiler_params=pltpu.CompilerParams(
            dimension_semantics=("arbitrary",), vmem_limit_bytes=VMEM_LIMIT),
        name="attn_sample",
    )(q, k_new, v_new, cft, cache_kt, cache_vt, cache_lft)


def _ffn_kernel(x_ref, at_ref, cv_ref, gst_ref, woa_ref, woc_ref, g1_ref, b1_ref,
                wug_ref, wuv_ref, fw_ref, fb_ref, wdn_ref, g2_ref, b2_ref,
                y_ref, gout_ref, gbuf, a_scr, *, s, tm, hdr, fc, alpha):
    t = pl.program_id(1)
    rs = tm // ROW_SPLITS
    groups = [slice(h * rs, (h + 1) * rs) for h in range(ROW_SPLITS)]
    mixed = [_dot(at_ref[0, r, :], woa_ref[...]) + _dot(cv_ref[0, r, :], woc_ref[...]) for r in groups]
    x1 = [_layer_norm(alpha * x_ref[0, r, :] + mx, g1_ref[...], b1_ref[...])
          for r, mx in zip(groups, mixed)]
    x1b = [v.astype(BF16) for v in x1]

    @pl.when(t == 0)
    def _():
        gbuf[hdr - 2 * s:hdr, :] = gst_ref[0]

    fw = fw_ref[...]
    fb = fb_ref[...]
    for c in range(wug_ref.shape[0]):
        cs = slice(c * fc, (c + 1) * fc)
        for h, r in enumerate(groups):
            g = _dot(x1b[h], wug_ref[c])
            val = _dot(x1b[h], wuv_ref[c])
            lo = hdr + h * rs
            gbuf[lo:lo + rs, cs] = g
            g1 = gbuf[lo - s:lo - s + rs, cs]
            g2 = gbuf[lo - 2 * s:lo - 2 * s + rs, cs]
            y = fb[:, cs] + fw[0:1, cs] * g2
            y = y + fw[1:2, cs] * g1
            y = y + fw[2:3, cs] * g
            a = y * (1.0 / (1.0 + jnp.exp(-y))) * val
            a_scr[r, cs] = a.astype(BF16)
    for h, r in enumerate(groups):
        down = _dot(a_scr[r, :], wdn_ref[...])
        y_ref[0, r, :] = _layer_norm(alpha * x1[h] + down, g2_ref[...], b2_ref[...])
    tail = gbuf[hdr + tm - 2 * s:hdr + tm, :]
    gout_ref[0] = tail
    gbuf[hdr - 2 * s:hdr, :] = tail


def _ffn(x, attn, conv, gst, w, l, s, alpha):
    ns, L, D = x.shape
    da = attn.shape[2]
    dc = conv.shape[2]
    nch, _, fc = w["wug"].shape[1:]
    dff = nch * fc
    tm = min(ROW_TILE, L)
    nt = L // tm
    hdr = max(SUBLANES, 2 * s)

    def wspec(shape):
        idx = (l,) + (0,) * len(shape)
        return pl.BlockSpec((None,) + shape, lambda n, t: idx, pipeline_mode=pl.Buffered(1))

    row = lambda width: pl.BlockSpec((1, tm, width), lambda n, t: (n, t, 0))
    return pl.pallas_call(
        functools.partial(_ffn_kernel, s=s, tm=tm, hdr=hdr, fc=fc, alpha=alpha),
        grid=(ns, nt),
        in_specs=[
            row(D), row(da), row(dc),
            pl.BlockSpec((1, 2 * s, dff), lambda n, t: (n, 0, 0)),
            wspec((da, D)), wspec((dc, D)), wspec((1, D)), wspec((1, D)),
            wspec((nch, D, fc)), wspec((nch, D, fc)), wspec((3, dff)), wspec((1, dff)),
            wspec((dff, D)), wspec((1, D)), wspec((1, D)),
        ],
        out_specs=[row(D), pl.BlockSpec((1, 2 * s, dff), lambda n, t: (n, 0, 0))],
        out_shape=[jax.ShapeDtypeStruct((ns, L, D), F32), jax.ShapeDtypeStruct((ns, 2 * s, dff), F32)],
        scratch_shapes=[pltpu.VMEM((hdr + tm, dff), F32), pltpu.VMEM((tm, dff), BF16)],
        compiler_params=pltpu.CompilerParams(
            dimension_semantics=("arbitrary", "arbitrary"), vmem_limit_bytes=VMEM_LIMIT),
        name="ffn",
    )(x, attn, conv, gst, w["woa"], w["woc"], w["g1"], w["b1"], w["wug"], w["wuv"],
      w["fw"], w["fb"], w["wdn"], w["g2"], w["b2"])


def _prep_weights(w_in, b_f, conv_w, conv_b, w_out, ln1_g, ln1_b, w_up, ffn_conv_w, ffn_conv_b,
                  w_down, ln2_g, ln2_b, dh):
    depth, D, _ = w_in.shape
    nh = b_f.shape[1]
    da = nh * dh
    dc = conv_w.shape[2]
    dff = ffn_conv_w.shape[2]
    assert dh == 64 and da % LANES == 0 and 3 * nh <= LANES and dff % FF_CHUNK == 0
    o = 3 * da + nh
    pad = LANES - 3 * nh
    wf = jnp.pad(jnp.tile(w_in[:, :, 3 * da:o], (1, 1, 3)), ((0, 0), (0, 0), (0, pad)))
    bf = jnp.pad(jnp.tile(b_f, (1, 3)), ((0, 0), (0, pad)))[:, None, :]
    nch = dff // FF_CHUNK

    def chunks(m):
        return m.reshape(depth, D, nch, FF_CHUNK).transpose(0, 2, 1, 3).astype(BF16)

    return dict(
        nh=nh,
        wq=(w_in[:, :, 0:da] * (dh ** -0.5)).astype(BF16),
        wk=w_in[:, :, da:2 * da].astype(BF16),
        wv=w_in[:, :, 2 * da:3 * da].astype(BF16),
        wf=wf.astype(BF16), bf=bf,
        wb=w_in[:, :, o:o + dc].astype(BF16),
        wc=w_in[:, :, o + dc:o + 2 * dc].astype(BF16),
        wh=w_in[:, :, o + 2 * dc:o + 3 * dc].astype(BF16),
        cw=conv_w, cb=conv_b[:, None, :],
        woa=w_out[:, 0:da].astype(BF16), woc=w_out[:, da:].astype(BF16),
        g1=ln1_g[:, None, :], b1=ln1_b[:, None, :],
        wug=chunks(w_up[:, :, 0:dff]), wuv=chunks(w_up[:, :, dff:]),
        fw=ffn_conv_w, fb=ffn_conv_b[:, None, :],
        wdn=w_down.astype(BF16), g2=ln2_g[:, None, :], b2=ln2_b[:, None, :],
    )


def kernel(x_prompt, x_sample, cache_k, cache_v, cache_logf, state_mix_conv, state_ffn_conv, w_in, b_f, conv_w, conv_b, w_out, ln1_g, ln1_b, w_up, ffn_conv_w, ffn_conv_b, w_down, ln2_g, ln2_b):
    depth = w_in.shape[0]
    B, T, D = x_prompt.shape
    DB, DT, _ = x_sample.shape
    P, nh, dh = cache_k.shape[2:]
    da = nh * dh
    dc = conv_w.shape[2]
    dff = ffn_conv_w.shape[2]
    alpha = (2 * depth) ** 0.25
    assert DB % SUBLANES == 0
    w = _prep_weights(w_in, b_f, conv_w, conv_b, w_out, ln1_g, ln1_b, w_up, ffn_conv_w, ffn_conv_b,
                      w_down, ln2_g, ln2_b, dh)

    ckt = cache_k.transpose(0, 1, 3, 4, 2).reshape(depth, DB, da, P)
    cvt = cache_v.transpose(0, 1, 3, 4, 2).reshape(depth, DB, da, P)
    clft = cache_logf.transpose(0, 1, 3, 2)

    def time_major(a):
        return a.transpose(0, 2, 1, 3).reshape(depth, 1, a.shape[2] * DB, a.shape[3])

    mix_tm = time_major(state_mix_conv)
    ffn_tm = time_major(state_ffn_conv)
    zero_mix = jnp.zeros((B, 2, dc), F32)
    zero_ffn = jnp.zeros((B, 2, dff), F32)

    yp = x_prompt
    ys = x_sample.transpose(1, 0, 2).reshape(1, DT * DB, D)
    kvp = kvs = None
    logf_p, mix_p, ffn_p, logf_s, mix_s, ffn_s = [], [], [], [], [], []
    for l in range(depth):
        q, kp, vp, lf, caug, _, conv, ust = _proj(yp, zero_mix, w, l, 1, kvp)
        kvp = (kp, vp)
        attn = _attn_prompt(q, kp, vp, caug, l, nh, dh)
        yp, gst = _ffn(yp, attn, conv, zero_ffn, w, l, 1, alpha)
        logf_p.append(lf); mix_p.append(ust); ffn_p.append(gst)

        q, ks, vs, lf, _, cf, conv, ust = _proj(ys, mix_tm[l], w, l, DB, kvs)
        kvs = (ks, vs)

        def batch_major(a):
            return a.reshape(-1, DB, a.shape[-1]).transpose(1, 0, 2)

        cft = batch_major(cf)[:, :, :nh].transpose(0, 2, 1)
        attn = _attn_sample(batch_major(q), batch_major(ks[l]), batch_major(vs[l]), cft,
                            ckt, cvt, clft, l, nh, dh)
        attn = attn.transpose(1, 0, 2).reshape(1, DT * DB, da)
        ys, gst = _ffn(ys, attn, conv, ffn_tm[l], w, l, DB, alpha)
        logf_s.append(batch_major(lf)); mix_s.append(batch_major(ust)); ffn_s.append(batch_major(gst))

    k_s = kvs[0].reshape(depth, DT, DB, nh, dh).transpose(0, 2, 1, 3, 4)
    v_s = kvs[1].reshape(depth, DT, DB, nh, dh).transpose(0, 2, 1, 3, 4)
    return (yp, ys.reshape(DT, DB, D).transpose(1, 0, 2),
            kvp[0].reshape(depth, B, T, nh, dh), kvp[1].reshape(depth, B, T, nh, dh),
            jnp.stack(logf_p), jnp.stack(mix_p), jnp.stack(ffn_p),
            k_s, v_s, jnp.stack(logf_s), jnp.stack(mix_s), jnp.stack(ffn_s))
```

```python
import functools

import jax
import jax.numpy as jnp
from jax import lax
from jax.experimental import pallas as pl
from jax.experimental.pallas import tpu as pltpu

F32 = jnp.float32
BF16 = jnp.bfloat16

LANES = 128
SUBLANES = 8
LN_EPS = 1e-5
LOG2E = 1.4426950408889634
ROW_TILE = 1024
ROW_SPLITS = 4
FFN_ROW_TILE = 1024
FFN_SPLITS = 4
Q_TILE = 256
FF_CHUNK = 256
QK_AHEAD = 6
KV_BLOCKS = 1
VMEM_LIMIT = 56 * 1024 * 1024


def _dot(a, b):
    return jnp.dot(a, b, preferred_element_type=F32)


def _dot_nt(a, b):
    return lax.dot_general(a, b, (((1,), (1,)), ((), ())), preferred_element_type=F32)


def _stride_cumsum(x, s):
    rows = x.shape[0]
    row = lax.broadcasted_iota(jnp.int32, x.shape, 0)
    d = s
    while d < rows:
        x = x + jnp.where(row >= d, pltpu.roll(x, d, axis=0), 0.0)
        d *= 2
    return x


def _split3(v, nh):
    hi = v.astype(BF16).astype(F32)
    r1 = v - hi
    mid = r1.astype(BF16).astype(F32)
    lo = (r1 - mid).astype(BF16).astype(F32)
    lane = lax.broadcasted_iota(jnp.int32, v.shape, 1)
    out = jnp.where(lane < nh, hi, jnp.where(lane < 2 * nh, mid, jnp.where(lane < 3 * nh, lo, 0.0)))
    return out.astype(BF16)


def _layer_norm(x, g, b):
    mu = jnp.mean(x, axis=-1, keepdims=True)
    xc = x - mu
    var = jnp.mean(xc * xc, axis=-1, keepdims=True)
    return xc * lax.rsqrt(var + LN_EPS) * g + b


def _head_query_t(qt, hh, head, nh, dh):
    row = lax.broadcasted_iota(jnp.int32, qt.shape, 0)
    qm = jnp.where((row >= hh * dh) & (row < (hh + 1) * dh), qt, jnp.zeros_like(qt))
    ones = jnp.where((row == head) | (row == head + nh) | (row == head + 2 * nh), 1.0, 0.0)
    return jnp.concatenate([qm, ones.astype(BF16)], axis=0)


def _proj_kernel(*refs, s, tm, hdr, nh, aliased, kv_t):
    (x_ref, st_ref, wq_ref, wk_ref, wv_ref, wf_ref, wb_ref, wc_ref, wh_ref,
     bf_ref, cw_ref, cb_ref) = refs[:12]
    rest = refs[14:] if aliased else refs[12:]
    (q_ref, k_ref, v_ref, logf_ref, caug_ref, cf_ref, conv_ref, ust_ref, ubuf, ccar) = rest
    t = pl.program_id(1)

    @pl.when(t == 0)
    def _():
        ccar[...] = jnp.zeros_like(ccar)
        ubuf[hdr - 2 * s:hdr, :] = st_ref[0]

    cw = cw_ref[...]
    rs = tm // ROW_SPLITS
    for h in range(ROW_SPLITS):
        r = slice(h * rs, (h + 1) * rs)
        xb = x_ref[0, r, :].astype(BF16)

        fl = _dot(xb, wf_ref[...]) + bf_ref[...]
        logf = jnp.minimum(fl, 0.0) - jnp.log1p(jnp.exp(-jnp.abs(fl)))
        logf_ref[0, r, :] = logf[:, :nh]
        cs = _stride_cumsum(logf, s)
        if s == 1:
            c = cs + ccar[0:1, :]
            ccar[...] = jnp.broadcast_to(c[rs - 1:rs, :], ccar.shape)
        else:
            c = cs + jnp.concatenate([ccar[...]] * (rs // s), axis=0)
            ccar[...] = c[rs - s:rs, :]
        cf_ref[0, r, :] = c
        caug_ref[0, r, :] = _split3(-LOG2E * c, nh)

        gb = _dot(xb, wb_ref[...])
        u = _dot(xb, wc_ref[...]) * _dot(xb, wh_ref[...])
        lo = hdr + h * rs
        ubuf[lo:lo + rs, :] = u
        u1 = ubuf[lo - s:lo - s + rs, :]
        u2 = ubuf[lo - 2 * s:lo - 2 * s + rs, :]
        conv = cb_ref[...] + cw[0:1, :] * u2
        conv = conv + cw[1:2, :] * u1
        conv = conv + cw[2:3, :] * u
        conv_ref[0, r, :] = (gb * conv).astype(BF16)

        q = _dot(xb, wq_ref[...])
        k = _dot(xb, wk_ref[...])
        v = _dot(xb, wv_ref[...])
        if kv_t:
            q_ref[0, :, r] = q.T.astype(BF16)
            k_ref[0, 0, :, r] = k.T
            v_ref[0, 0, :, r] = v.T
        else:
            q_ref[0, r, :] = q.astype(BF16)
            k_ref[0, 0, r, :] = k
            v_ref[0, 0, r, :] = v
    tail = ubuf[hdr + tm - 2 * s:hdr + tm, :]
    ust_ref[0] = tail
    ubuf[hdr - 2 * s:hdr, :] = tail


def _proj(x, st, w, l, s, kv_bufs, kv_t):
    ns, L, D = x.shape
    depth = w["wq"].shape[0]
    da = w["wq"].shape[2]
    dc = w["wb"].shape[2]
    nh = w["nh"]
    tm = min(ROW_TILE, L)
    nt = L // tm
    assert L % tm == 0 and tm % ROW_SPLITS == 0 and (tm // ROW_SPLITS) % max(s, SUBLANES) == 0
    hdr = max(SUBLANES, 2 * s)
    aliased = kv_bufs is not None

    def wspec(shape):
        return pl.BlockSpec((None,) + shape, lambda n, t: (l, 0, 0), pipeline_mode=pl.Buffered(1))

    in_specs = [
        pl.BlockSpec((1, tm, D), lambda n, t: (n, t, 0)),
        pl.BlockSpec((1, 2 * s, dc), lambda n, t: (n, 0, 0)),
        wspec((D, da)), wspec((D, da)), wspec((D, da)), wspec((D, LANES)),
        wspec((D, dc)), wspec((D, dc)), wspec((D, dc)),
        wspec((1, LANES)), wspec((3, dc)), wspec((1, dc)),
    ]
    args = [x, st, w["wq"], w["wk"], w["wv"], w["wf"], w["wb"], w["wc"], w["wh"],
            w["bf"], w["cw"], w["cb"]]
    io_alias = {}
    if aliased:
        in_specs += [pl.BlockSpec(memory_space=pl.ANY), pl.BlockSpec(memory_space=pl.ANY)]
        args += list(kv_bufs)
        io_alias = {12: 1, 13: 2}
    if kv_t:
        q_shape = jax.ShapeDtypeStruct((ns, da, L), BF16)
        q_spec = pl.BlockSpec((1, da, tm), lambda n, t: (n, 0, t))
        kv_shape = jax.ShapeDtypeStruct((depth, ns, da, L), F32)
        kv_spec = pl.BlockSpec((1, 1, da, tm), lambda n, t: (l, n, 0, t))
    else:
        q_shape = jax.ShapeDtypeStruct((ns, L, da), BF16)
        q_spec = pl.BlockSpec((1, tm, da), lambda n, t: (n, t, 0))
        kv_shape = jax.ShapeDtypeStruct((depth, ns, L, da), F32)
        kv_spec = pl.BlockSpec((1, 1, tm, da), lambda n, t: (l, n, t, 0))
    out_shape = [
        q_shape, kv_shape, kv_shape,
        jax.ShapeDtypeStruct((ns, L, nh), F32),
        jax.ShapeDtypeStruct((ns, L, LANES), BF16),
        jax.ShapeDtypeStruct((ns, L, LANES), F32),
        jax.ShapeDtypeStruct((ns, L, dc), BF16),
        jax.ShapeDtypeStruct((ns, 2 * s, dc), F32),
    ]
    out_specs = [
        q_spec, kv_spec, kv_spec,
        pl.BlockSpec((1, tm, nh), lambda n, t: (n, t, 0)),
        pl.BlockSpec((1, tm, LANES), lambda n, t: (n, t, 0)),
        pl.BlockSpec((1, tm, LANES), lambda n, t: (n, t, 0)),
        pl.BlockSpec((1, tm, dc), lambda n, t: (n, t, 0)),
        pl.BlockSpec((1, 2 * s, dc), lambda n, t: (n, 0, 0)),
    ]
    return pl.pallas_call(
        functools.partial(_proj_kernel, s=s, tm=tm, hdr=hdr, nh=nh, aliased=aliased, kv_t=kv_t),
        grid=(ns, nt),
        in_specs=in_specs, out_specs=out_specs, out_shape=out_shape,
        scratch_shapes=[pltpu.VMEM((hdr + tm, dc), F32), pltpu.VMEM((max(s, SUBLANES), LANES), F32)],
        input_output_aliases=io_alias,
        compiler_params=pltpu.CompilerParams(
            dimension_semantics=("arbitrary", "arbitrary"), vmem_limit_bytes=VMEM_LIMIT),
        name="proj",
    )(*args)


def _attn_prompt_kernel(q_ref, k_ref, v_ref, caug_ref, o_ref, kaug, vt, qa_scr, *, T, tq, nh, dh):
    p = pl.program_id(1)
    kaug[:, 0:LANES] = k_ref[0, 0].T.astype(BF16)
    kaug[:, LANES:2 * LANES] = caug_ref[0]
    for hh in range(2):
        vt[hh, 0:dh, :] = v_ref[0, 0, hh * dh:(hh + 1) * dh, :].astype(BF16)
        vt[hh, dh:, :] = jnp.ones((vt.shape[1] - dh, T), BF16)
    row = lax.broadcasted_iota(jnp.int32, (tq, tq), 0)
    col = lax.broadcasted_iota(jnp.int32, (tq, tq), 1)
    causal = row <= col
    nq = T // tq
    for c in range(nq):
        qt = q_ref[0, :, c * tq:(c + 1) * tq]
        for hh in range(2):
            qa_scr[c, hh] = _head_query_t(qt, hh, 2 * p + hh, nh, dh)
    streams = [[(c, hh, r0, min(r0 + KV_BLOCKS, c + 1)) for r0 in range(0, c + 1, KV_BLOCKS)]
               for c in range(nq) for hh in range(2)]
    units = [st[r] for r in range(max(map(len, streams))) for st in streams if r < len(st)]

    def scores(u):
        c, hh, r0, r1 = u
        return _dot(kaug[r0 * tq:r1 * tq, :], qa_scr[c, hh])

    pending = {k: scores(units[k]) for k in range(min(QK_AHEAD, len(units)))}
    state, outs = {}, {}
    for k, (c, hh, r0, r1) in enumerate(units):
        s = pending.pop(k)
        if k + QK_AHEAD < len(units):
            pending[k + QK_AHEAD] = scores(units[k + QK_AHEAD])
        last = r1 == c + 1
        if last:
            sd = jnp.where(causal, s[-tq:, :], -jnp.inf)
            s = sd if r1 - r0 == 1 else jnp.concatenate([s[:-tq, :], sd], axis=0)
        vj = vt[hh, :, r0 * tq:r1 * tq]
        mc = jnp.max(s, axis=0, keepdims=True)
        if r0 == 0:
            m = mc
            acc = _dot(vj, jnp.exp2(s - m).astype(BF16))
        else:
            m0, a0 = state.pop((c, hh))
            m = jnp.maximum(m0, mc)
            acc = jnp.exp2(m0 - m) * a0 + _dot(vj, jnp.exp2(s - m).astype(BF16))
        if last:
            outs[hh] = acc[0:dh, :] / acc[dh:dh + 1, :]
            if hh == 1:
                o_ref[0, c * tq:(c + 1) * tq, :] = jnp.concatenate(
                    [outs[0], outs[1]], axis=0).T.astype(BF16)
        else:
            state[c, hh] = (m, acc)


def _attn_prompt(q_t, k_all, v_all, caug, l, nh, dh):
    b, da, T = q_t.shape
    tq = min(Q_TILE, T)
    npair = da // LANES
    return pl.pallas_call(
        functools.partial(_attn_prompt_kernel, T=T, tq=tq, nh=nh, dh=dh),
        grid=(b, npair),
        in_specs=[
            pl.BlockSpec((1, LANES, T), lambda n, p: (n, p, 0)),
            pl.BlockSpec((1, 1, LANES, T), lambda n, p: (l, n, p, 0)),
            pl.BlockSpec((1, 1, LANES, T), lambda n, p: (l, n, p, 0)),
            pl.BlockSpec((1, T, LANES), lambda n, p: (n, 0, 0)),
        ],
        out_specs=pl.BlockSpec((1, T, LANES), lambda n, p: (n, 0, p)),
        out_shape=jax.ShapeDtypeStruct((b, T, da), BF16),
        scratch_shapes=[pltpu.VMEM((T, 2 * LANES), BF16), pltpu.VMEM((2, dh + 16, T), BF16),
                        pltpu.VMEM((T // tq, 2, 2 * LANES, tq), BF16)],
        compiler_params=pltpu.CompilerParams(
            dimension_semantics=("arbitrary", "arbitrary"), vmem_limit_bytes=VMEM_LIMIT),
        name="attn_prompt",
    )(q_t, k_all, v_all, caug)


def _lane_cumsum(x):
    n = x.shape[1]
    lane = lax.broadcasted_iota(jnp.int32, x.shape, 1)
    d = 1
    while d < n:
        x = x + jnp.where(lane >= d, pltpu.roll(x, d, axis=1), 0.0)
        d *= 2
    return x


def _attn_sample_kernel(q_ref, kn_ref, vn_ref, cft_ref, ck_ref, cv_ref, clf_ref, o_ref,
                        *, P, dt, nh, dh):
    cc = _lane_cumsum(clf_ref[...])
    cn = LOG2E * (cc[:, P - 1:P] + cft_ref[0])
    cc = LOG2E * cc
    lane = lax.broadcasted_iota(jnp.int32, (dt, LANES), 1)
    row = lax.broadcasted_iota(jnp.int32, (2 * dt, dt), 0)
    col = lax.broadcasted_iota(jnp.int32, (2 * dt, dt), 1)
    causal = col <= jnp.where(row < dt, row, row - dt)
    for p in range(ck_ref.shape[0] // LANES):
        pr = slice(p * LANES, (p + 1) * LANES)
        kt = ck_ref[pr, :].astype(BF16)
        vt = cv_ref[pr, :].astype(BF16)
        qt = q_ref[0, :, pr]
        q2 = jnp.concatenate([jnp.where(lane < dh, qt, jnp.zeros_like(qt)),
                              jnp.where(lane >= dh, qt, jnp.zeros_like(qt))], axis=0)

        def per_head(c, width):
            return jnp.concatenate([jnp.broadcast_to(c[2 * p:2 * p + 1, :], (dt, width)),
                                    jnp.broadcast_to(c[2 * p + 1:2 * p + 2, :], (dt, width))], axis=0)

        sc = _dot(q2, kt) - per_head(cc, P)
        sn = _dot_nt(q2, kn_ref[0, :, pr].astype(BF16)) - per_head(cn, dt)
        sn = jnp.where(causal, sn, -jnp.inf)
        m = jnp.maximum(jnp.max(sc, axis=1, keepdims=True), jnp.max(sn, axis=1, keepdims=True))
        pc = jnp.exp2(sc - m)
        pn = jnp.exp2(sn - m)
        lsum = jnp.sum(pc, axis=1, keepdims=True) + jnp.sum(pn, axis=1, keepdims=True)
        o = _dot_nt(pc.astype(BF16), vt) + _dot(pn.astype(BF16), vn_ref[0, :, pr].astype(BF16))
        o = o / lsum
        o_ref[0, :, pr] = jnp.where(lane < dh, o[0:dt, :], o[dt:2 * dt, :]).astype(BF16)


def _attn_sample(q, k_new, v_new, cft, cache_kt, cache_vt, cache_lft, l, nh, dh):
    db, dt, da = q.shape
    P = cache_kt.shape[3]
    assert dt % 16 == 0
    tok = lambda width: pl.BlockSpec((1, dt, width), lambda n: (n, 0, 0))
    return pl.pallas_call(
        functools.partial(_attn_sample_kernel, P=P, dt=dt, nh=nh, dh=dh),
        grid=(db,),
        in_specs=[
            tok(da), tok(da), tok(da),
            pl.BlockSpec((1, nh, dt), lambda n: (n, 0, 0)),
            pl.BlockSpec((None, None, da, P), lambda n: (l, n, 0, 0)),
            pl.BlockSpec((None, None, da, P), lambda n: (l, n, 0, 0)),
            pl.BlockSpec((None, None, nh, P), lambda n: (l, n, 0, 0)),
        ],
        out_specs=tok(da),
        out_shape=jax.ShapeDtypeStruct((db, dt, da), BF16),
        compiler_params=pltpu.CompilerParams(
            dimension_semantics=("arbitrary",), vmem_limit_bytes=VMEM_LIMIT),
        name="attn_sample",
    )(q, k_new, v_new, cft, cache_kt, cache_vt, cache_lft)


def _ffn_kernel(x_ref, at_ref, cv_ref, gst_ref, woa_ref, woc_ref, g1_ref, b1_ref,
                wug_ref, wuv_ref, fw_ref, fb_ref, wdn_ref, g2_ref, b2_ref,
                y_ref, gout_ref, gbuf, hbuf, a_scr, *, s, tm, hdr, fc, alpha):
    t = pl.program_id(1)

    @pl.when(t == 0)
    def _():
        hbuf[...] = gst_ref[0]

    rs = tm // FFN_SPLITS
    groups = [slice(h * rs, (h + 1) * rs) for h in range(FFN_SPLITS)]
    mixed = [_dot(at_ref[0, r, :], woa_ref[...]) + _dot(cv_ref[0, r, :], woc_ref[...]) for r in groups]
    x1 = [_layer_norm(alpha * x_ref[0, r, :] + mx, g1_ref[...], b1_ref[...])
          for r, mx in zip(groups, mixed)]
    x1b = [v.astype(BF16) for v in x1]

    fw = fw_ref[...]
    fb = fb_ref[...]
    for c in range(wug_ref.shape[0]):
        cs = slice(c * fc, (c + 1) * fc)
        gb = gbuf.at[c % 2]
        gb[hdr - 2 * s:hdr, :] = hbuf[:, cs]
        for h, r in enumerate(groups):
            g = _dot(x1b[h], wug_ref[c])
            val = _dot(x1b[h], wuv_ref[c])
            lo = hdr + h * rs
            gb[lo:lo + rs, :] = g
            g1 = gb[lo - s:lo - s + rs, :]
            g2 = gb[lo - 2 * s:lo - 2 * s + rs, :]
            y = fb[:, cs] + fw[0:1, cs] * g2
            y = y + fw[1:2, cs] * g1
            y = y + fw[2:3, cs] * g
            a = y * (1.0 / (1.0 + jnp.exp(-y))) * val
            a_scr[r, cs] = a.astype(BF16)
        hbuf[:, cs] = gb[hdr + tm - 2 * s:hdr + tm, :]
    for h, r in enumerate(groups):
        down = _dot(a_scr[r, :], wdn_ref[...])
        y_ref[0, r, :] = _layer_norm(alpha * x1[h] + down, g2_ref[...], b2_ref[...])
    gout_ref[0] = hbuf[...]


def _ffn(x, attn, conv, gst, w, l, s, alpha):
    ns, L, D = x.shape
    da = attn.shape[2]
    dc = conv.shape[2]
    nch, _, fc = w["wug"].shape[1:]
    dff = nch * fc
    tm = min(FFN_ROW_TILE, L)
    nt = L // tm
    assert L % tm == 0 and (tm // FFN_SPLITS) % max(s, SUBLANES) == 0
    hdr = max(SUBLANES, 2 * s)

    def wspec(shape):
        idx = (l,) + (0,) * len(shape)
        return pl.BlockSpec((None,) + shape, lambda n, t: idx, pipeline_mode=pl.Buffered(1))

    row = lambda width: pl.BlockSpec((1, tm, width), lambda n, t: (n, t, 0))
    return pl.pallas_call(
        functools.partial(_ffn_kernel, s=s, tm=tm, hdr=hdr, fc=fc, alpha=alpha),
        grid=(ns, nt),
        in_specs=[
            row(D), row(da), row(dc),
            pl.BlockSpec((1, 2 * s, dff), lambda n, t: (n, 0, 0)),
            wspec((da, D)), wspec((dc, D)), wspec((1, D)), wspec((1, D)),
            wspec((nch, D, fc)), wspec((nch, D, fc)), wspec((3, dff)), wspec((1, dff)),
            wspec((dff, D)), wspec((1, D)), wspec((1, D)),
        ],
        out_specs=[row(D), pl.BlockSpec((1, 2 * s, dff), lambda n, t: (n, 0, 0))],
        out_shape=[jax.ShapeDtypeStruct((ns, L, D), F32), jax.ShapeDtypeStruct((ns, 2 * s, dff), F32)],
        scratch_shapes=[pltpu.VMEM((2, hdr + tm, fc), F32), pltpu.VMEM((2 * s, dff), F32),
                        pltpu.VMEM((tm, dff), BF16)],
        compiler_params=pltpu.CompilerParams(
            dimension_semantics=("arbitrary", "arbitrary"), vmem_limit_bytes=VMEM_LIMIT),
        name="ffn",
    )(x, attn, conv, gst, w["woa"], w["woc"], w["g1"], w["b1"], w["wug"], w["wuv"],
      w["fw"], w["fb"], w["wdn"], w["g2"], w["b2"])


def _prep_weights(w_in, b_f, conv_w, conv_b, w_out, ln1_g, ln1_b, w_up, ffn_conv_w, ffn_conv_b,
                  w_down, ln2_g, ln2_b, dh):
    depth, D, _ = w_in.shape
    nh = b_f.shape[1]
    da = nh * dh
    dc = conv_w.shape[2]
    dff = ffn_conv_w.shape[2]
    assert dh == 64 and da % LANES == 0 and 3 * nh <= LANES and dff % FF_CHUNK == 0
    o = 3 * da + nh
    pad = LANES - 3 * nh
    wf = jnp.pad(jnp.tile(w_in[:, :, 3 * da:o], (1, 1, 3)), ((0, 0), (0, 0), (0, pad)))
    bf = jnp.pad(jnp.tile(b_f, (1, 3)), ((0, 0), (0, pad)))[:, None, :]
    nch = dff // FF_CHUNK

    def chunks(m):
        return m.reshape(depth, D, nch, FF_CHUNK).transpose(0, 2, 1, 3).astype(BF16)

    return dict(
        nh=nh,
        wq=(w_in[:, :, 0:da] * (dh ** -0.5 * LOG2E)).astype(BF16),
        wk=w_in[:, :, da:2 * da].astype(BF16),
        wv=w_in[:, :, 2 * da:3 * da].astype(BF16),
        wf=wf.astype(BF16), bf=bf,
        wb=w_in[:, :, o:o + dc].astype(BF16),
        wc=w_in[:, :, o + dc:o + 2 * dc].astype(BF16),
        wh=w_in[:, :, o + 2 * dc:o + 3 * dc].astype(BF16),
        cw=conv_w, cb=conv_b[:, None, :],
        woa=w_out[:, 0:da].astype(BF16), woc=w_out[:, da:].astype(BF16),
        g1=ln1_g[:, None, :], b1=ln1_b[:, None, :],
        wug=chunks(w_up[:, :, 0:dff]), wuv=chunks(w_up[:, :, dff:]),
        fw=ffn_conv_w, fb=ffn_conv_b[:, None, :],
        wdn=w_down.astype(BF16), g2=ln2_g[:, None, :], b2=ln2_b[:, None, :],
    )


def kernel(x_prompt, x_sample, cache_k, cache_v, cache_logf, state_mix_conv, state_ffn_conv, w_in, b_f, conv_w, conv_b, w_out, ln1_g, ln1_b, w_up, ffn_conv_w, ffn_conv_b, w_down, ln2_g, ln2_b):
    depth = w_in.shape[0]
    B, T, D = x_prompt.shape
    DB, DT, _ = x_sample.shape
    P, nh, dh = cache_k.shape[2:]
    da = nh * dh
    dc = conv_w.shape[2]
    dff = ffn_conv_w.shape[2]
    alpha = (2 * depth) ** 0.25
    assert DB % SUBLANES == 0
    w = _prep_weights(w_in, b_f, conv_w, conv_b, w_out, ln1_g, ln1_b, w_up, ffn_conv_w, ffn_conv_b,
                      w_down, ln2_g, ln2_b, dh)

    ckt = cache_k.transpose(0, 1, 3, 4, 2).reshape(depth, DB, da, P)
    cvt = cache_v.transpose(0, 1, 3, 4, 2).reshape(depth, DB, da, P)
    clft = cache_logf.transpose(0, 1, 3, 2)

    def time_major(a):
        return a.transpose(0, 2, 1, 3).reshape(depth, 1, a.shape[2] * DB, a.shape[3])

    mix_tm = time_major(state_mix_conv)
    ffn_tm = time_major(state_ffn_conv)
    zero_mix = jnp.zeros((B, 2, dc), F32)
    zero_ffn = jnp.zeros((B, 2, dff), F32)

    yp = x_prompt
    ys = x_sample.transpose(1, 0, 2).reshape(1, DT * DB, D)
    kvp = kvs = None
    logf_p, mix_p, ffn_p, logf_s, mix_s, ffn_s = [], [], [], [], [], []
    for l in range(depth):
        q, kp, vp, lf, caug, _, conv, ust = _proj(yp, zero_mix, w, l, 1, kvp, True)
        kvp = (kp, vp)
        attn = _attn_prompt(q, kp, vp, caug, l, nh, dh)
        yp, gst = _ffn(yp, attn, conv, zero_ffn, w, l, 1, alpha)
        logf_p.append(lf); mix_p.append(ust); ffn_p.append(gst)

        q, ks, vs, lf, _, cf, conv, ust = _proj(ys, mix_tm[l], w, l, DB, kvs, False)
        kvs = (ks, vs)

        def batch_major(a):
            return a.reshape(-1, DB, a.shape[-1]).transpose(1, 0, 2)

        cft = batch_major(cf)[:, :, :nh].transpose(0, 2, 1)
        attn = _attn_sample(batch_major(q), batch_major(ks[l]), batch_major(vs[l]), cft,
                            ckt, cvt, clft, l, nh, dh)
        attn = attn.transpose(1, 0, 2).reshape(1, DT * DB, da)
        ys, gst = _ffn(ys, attn, conv, ffn_tm[l], w, l, DB, alpha)
        logf_s.append(batch_major(lf)); mix_s.append(batch_major(ust)); ffn_s.append(batch_major(gst))

    k_s = kvs[0].reshape(depth, DT, DB, nh, dh).transpose(0, 2, 1, 3, 4)
    v_s = kvs[1].reshape(depth, DT, DB, nh, dh).transpose(0, 2, 1, 3, 4)
    return (yp, ys.reshape(DT, DB, D).transpose(1, 0, 2),
            kvp[0].reshape(depth, B, nh, dh, T).transpose(0, 1, 4, 2, 3),
            kvp[1].reshape(depth, B, nh, dh, T).transpose(0, 1, 4, 2, 3),
            jnp.stack(logf_p), jnp.stack(mix_p), jnp.stack(ffn_p),
            k_s, v_s, jnp.stack(logf_s), jnp.stack(mix_s), jnp.stack(ffn_s))
```

```python
import functools

import jax
import jax.numpy as jnp
from jax import lax
from jax.experimental import pallas as pl
from jax.experimental.pallas import tpu as pltpu

F32 = jnp.float32
BF16 = jnp.bfloat16

LANES = 128
SUBLANES = 8
BF16_ROWS = 16
LN_EPS = 1e-5
LOG2E = 1.4426950408889634
ROW_TILE = 1024
ROW_SPLITS = 4
FFN_ROW_TILE = 1024
FFN_SPLITS = 4
Q_TILE = 256
FF_CHUNK = 256
QK_AHEAD = 6
KV_BLOCKS = 1
VMEM_LIMIT = 56 * 1024 * 1024


def _dot(a, b):
    return jnp.dot(a, b, preferred_element_type=F32)


def _dot_nt(a, b):
    return lax.dot_general(a, b, (((1,), (1,)), ((), ())), preferred_element_type=F32)


def _stride_cumsum(x, s):
    rows = x.shape[0]
    row = lax.broadcasted_iota(jnp.int32, x.shape, 0)
    d = s
    while d < rows:
        x = x + jnp.where(row >= d, pltpu.roll(x, d, axis=0), 0.0)
        d *= 2
    return x


def _split3(v, nh):
    hi = v.astype(BF16).astype(F32)
    r1 = v - hi
    mid = r1.astype(BF16).astype(F32)
    lo = (r1 - mid).astype(BF16).astype(F32)
    lane = lax.broadcasted_iota(jnp.int32, v.shape, 1)
    out = jnp.where(lane < nh, hi, jnp.where(lane < 2 * nh, mid, jnp.where(lane < 3 * nh, lo, 0.0)))
    return out.astype(BF16)


def _layer_norm(x, g, b):
    mu = jnp.mean(x, axis=-1, keepdims=True)
    xc = x - mu
    var = jnp.mean(xc * xc, axis=-1, keepdims=True)
    return xc * lax.rsqrt(var + LN_EPS) * g + b


def _head_query_t(qt, hh, head, nh, dh):
    row = lax.broadcasted_iota(jnp.int32, qt.shape, 0)
    qm = jnp.where((row >= hh * dh) & (row < (hh + 1) * dh), qt, jnp.zeros_like(qt))
    ones = jnp.where((row == head) | (row == head + nh) | (row == head + 2 * nh), 1.0, 0.0)
    return jnp.concatenate([qm, ones.astype(BF16)], axis=0)


def _proj_kernel(*refs, s, tm, hdr, nh, aliased, kv_t):
    (x_ref, st_ref, wq_ref, wk_ref, wv_ref, wf_ref, wb_ref, wc_ref, wh_ref,
     bf_ref, cw_ref, cb_ref) = refs[:12]
    rest = refs[14:] if aliased else refs[12:]
    (q_ref, k_ref, v_ref, logf_ref, caug_ref, conv_ref, ust_ref) = rest[:7]
    cf_ref = None if kv_t else rest[7]
    ubuf, ccar = rest[-2:]
    t = pl.program_id(1)

    @pl.when(t == 0)
    def _():
        ccar[...] = jnp.zeros_like(ccar)
        ubuf[hdr - 2 * s:hdr, :] = st_ref[0]

    cw = cw_ref[...]
    rs = tm // ROW_SPLITS
    for h in range(ROW_SPLITS):
        r = slice(h * rs, (h + 1) * rs)
        xb = x_ref[0, r, :].astype(BF16)

        fl = _dot(xb, wf_ref[...]) + bf_ref[...]
        logf = jnp.minimum(fl, 0.0) - jnp.log1p(jnp.exp(-jnp.abs(fl)))
        if kv_t:
            logf_ref[0, :, r] = logf.T[:nh, :]
        else:
            logf_ref[0, r, :] = logf[:, :nh]
        cs = _stride_cumsum(logf, s)
        if s == 1:
            c = cs + ccar[0:1, :]
            ccar[...] = jnp.broadcast_to(c[rs - 1:rs, :], ccar.shape)
        else:
            c = cs + jnp.concatenate([ccar[...]] * (rs // s), axis=0)
            ccar[...] = c[rs - s:rs, :]
        if cf_ref is not None:
            cf_ref[0, r, :] = c
        caug_ref[0, r, :] = _split3(-LOG2E * c, nh)

        gb = _dot(xb, wb_ref[...])
        u = _dot(xb, wc_ref[...]) * _dot(xb, wh_ref[...])
        lo = hdr + h * rs
        ubuf[lo:lo + rs, :] = u
        u1 = ubuf[lo - s:lo - s + rs, :]
        u2 = ubuf[lo - 2 * s:lo - 2 * s + rs, :]
        conv = cb_ref[...] + cw[0:1, :] * u2
        conv = conv + cw[1:2, :] * u1
        conv = conv + cw[2:3, :] * u
        conv_ref[0, r, :] = (gb * conv).astype(BF16)

        q = _dot(xb, wq_ref[...])
        k = _dot(xb, wk_ref[...])
        v = _dot(xb, wv_ref[...])
        if kv_t:
            q_ref[0, :, r] = q.T.astype(BF16)
            k_ref[0, 0, :, r] = k.T
            v_ref[0, 0, :, r] = v.T
        else:
            q_ref[0, r, :] = q.astype(BF16)
            k_ref[0, 0, r, :] = k
            v_ref[0, 0, r, :] = v
    tail = ubuf[hdr + tm - 2 * s:hdr + tm, :]
    ust_ref[0] = tail
    ubuf[hdr - 2 * s:hdr, :] = tail


def _proj(x, st, w, l, s, kv_bufs, kv_t):
    ns, L, D = x.shape
    depth = w["wq"].shape[0]
    da = w["wq"].shape[2]
    dc = w["wb"].shape[2]
    nh = w["nh"]
    tm = min(ROW_TILE, L)
    nt = L // tm
    assert L % tm == 0 and tm % ROW_SPLITS == 0 and (tm // ROW_SPLITS) % max(s, SUBLANES) == 0
    hdr = max(SUBLANES, 2 * s)
    aliased = kv_bufs is not None

    def wspec(shape):
        return pl.BlockSpec((None,) + shape, lambda n, t: (l, 0, 0), pipeline_mode=pl.Buffered(1))

    in_specs = [
        pl.BlockSpec((1, tm, D), lambda n, t: (n, t, 0)),
        pl.BlockSpec((1, 2 * s, dc), lambda n, t: (n, 0, 0)),
        wspec((D, da)), wspec((D, da)), wspec((D, da)), wspec((D, LANES)),
        wspec((D, dc)), wspec((D, dc)), wspec((D, dc)),
        wspec((1, LANES)), wspec((3, dc)), wspec((1, dc)),
    ]
    args = [x, st, w["wq"], w["wk"], w["wv"], w["wf"], w["wb"], w["wc"], w["wh"],
            w["bf"], w["cw"], w["cb"]]
    io_alias = {}
    if aliased:
        in_specs += [pl.BlockSpec(memory_space=pl.ANY), pl.BlockSpec(memory_space=pl.ANY)]
        args += list(kv_bufs)
        io_alias = {12: 1, 13: 2}
    if kv_t:
        q_shape = jax.ShapeDtypeStruct((ns, da, L), BF16)
        q_spec = pl.BlockSpec((1, da, tm), lambda n, t: (n, 0, t))
        kv_shape = jax.ShapeDtypeStruct((depth, ns, da, L), F32)
        kv_spec = pl.BlockSpec((1, 1, da, tm), lambda n, t: (l, n, 0, t))
        lf_shape = jax.ShapeDtypeStruct((ns, nh, L), F32)
        lf_spec = pl.BlockSpec((1, nh, tm), lambda n, t: (n, 0, t))
    else:
        q_shape = jax.ShapeDtypeStruct((ns, L, da), BF16)
        q_spec = pl.BlockSpec((1, tm, da), lambda n, t: (n, t, 0))
        kv_shape = jax.ShapeDtypeStruct((depth, ns, L, da), F32)
        kv_spec = pl.BlockSpec((1, 1, tm, da), lambda n, t: (l, n, t, 0))
        lf_shape = jax.ShapeDtypeStruct((ns, L, nh), F32)
        lf_spec = pl.BlockSpec((1, tm, nh), lambda n, t: (n, t, 0))
    out_shape = [
        q_shape, kv_shape, kv_shape, lf_shape,
        jax.ShapeDtypeStruct((ns, L, LANES), BF16),
        jax.ShapeDtypeStruct((ns, L, dc), BF16),
        jax.ShapeDtypeStruct((ns, 2 * s, dc), F32),
    ]
    out_specs = [
        q_spec, kv_spec, kv_spec, lf_spec,
        pl.BlockSpec((1, tm, LANES), lambda n, t: (n, t, 0)),
        pl.BlockSpec((1, tm, dc), lambda n, t: (n, t, 0)),
        pl.BlockSpec((1, 2 * s, dc), lambda n, t: (n, 0, 0)),
    ]
    if not kv_t:
        out_shape.append(jax.ShapeDtypeStruct((ns, L, LANES), F32))
        out_specs.append(pl.BlockSpec((1, tm, LANES), lambda n, t: (n, t, 0)))
    return pl.pallas_call(
        functools.partial(_proj_kernel, s=s, tm=tm, hdr=hdr, nh=nh, aliased=aliased, kv_t=kv_t),
        grid=(ns, nt),
        in_specs=in_specs, out_specs=out_specs, out_shape=out_shape,
        scratch_shapes=[pltpu.VMEM((hdr + tm, dc), F32), pltpu.VMEM((max(s, SUBLANES), LANES), F32)],
        input_output_aliases=io_alias,
        compiler_params=pltpu.CompilerParams(
            dimension_semantics=("arbitrary", "arbitrary"), vmem_limit_bytes=VMEM_LIMIT),
        name="proj",
    )(*args)


def _attn_prompt_kernel(q_ref, k_ref, v_ref, caug_ref, o_ref, kaug, vt, qa_scr, *, T, tq, nh, dh):
    p = pl.program_id(1)
    kaug[:, 0:LANES] = k_ref[0, 0].T.astype(BF16)
    kaug[:, LANES:2 * LANES] = caug_ref[0]
    for hh in range(2):
        vt[hh, 0:dh, :] = v_ref[0, 0, hh * dh:(hh + 1) * dh, :].astype(BF16)
        vt[hh, dh:, :] = jnp.ones((vt.shape[1] - dh, T), BF16)
    row = lax.broadcasted_iota(jnp.int32, (tq, tq), 0)
    col = lax.broadcasted_iota(jnp.int32, (tq, tq), 1)
    causal = row <= col
    nq = T // tq
    for c in range(nq):
        qt = q_ref[0, :, c * tq:(c + 1) * tq]
        for hh in range(2):
            qa_scr[c, hh] = _head_query_t(qt, hh, 2 * p + hh, nh, dh)
    streams = [[(c, hh, r0, min(r0 + KV_BLOCKS, c + 1)) for r0 in range(0, c + 1, KV_BLOCKS)]
               for c in range(nq) for hh in range(2)]
    units = [st[r] for r in range(max(map(len, streams))) for st in streams if r < len(st)]

    def scores(u):
        c, hh, r0, r1 = u
        return _dot(kaug[r0 * tq:r1 * tq, :], qa_scr[c, hh])

    pending = {k: scores(units[k]) for k in range(min(QK_AHEAD, len(units)))}
    state, outs = {}, {}
    for k, (c, hh, r0, r1) in enumerate(units):
        s = pending.pop(k)
        if k + QK_AHEAD < len(units):
            pending[k + QK_AHEAD] = scores(units[k + QK_AHEAD])
        last = r1 == c + 1
        if last:
            sd = jnp.where(causal, s[-tq:, :], -jnp.inf)
            s = sd if r1 - r0 == 1 else jnp.concatenate([s[:-tq, :], sd], axis=0)
        vj = vt[hh, :, r0 * tq:r1 * tq]
        mc = jnp.max(s, axis=0, keepdims=True)
        if r0 == 0:
            m = mc
            acc = _dot(vj, jnp.exp2(s - m).astype(BF16))
        else:
            m0, a0 = state.pop((c, hh))
            m = jnp.maximum(m0, mc)
            acc = jnp.exp2(m0 - m) * a0 + _dot(vj, jnp.exp2(s - m).astype(BF16))
        if last:
            outs[hh] = acc[0:dh, :] / acc[dh:dh + 1, :]
            if hh == 1:
                o_ref[0, c * tq:(c + 1) * tq, :] = jnp.concatenate(
                    [outs[0], outs[1]], axis=0).T.astype(BF16)
        else:
            state[c, hh] = (m, acc)


def _attn_prompt(q_t, k_all, v_all, caug, l, nh, dh):
    b, da, T = q_t.shape
    tq = min(Q_TILE, T)
    npair = da // LANES
    return pl.pallas_call(
        functools.partial(_attn_prompt_kernel, T=T, tq=tq, nh=nh, dh=dh),
        grid=(b, npair),
        in_specs=[
            pl.BlockSpec((1, LANES, T), lambda n, p: (n, p, 0)),
            pl.BlockSpec((1, 1, LANES, T), lambda n, p: (l, n, p, 0)),
            pl.BlockSpec((1, 1, LANES, T), lambda n, p: (l, n, p, 0)),
            pl.BlockSpec((1, T, LANES), lambda n, p: (n, 0, 0)),
        ],
        out_specs=pl.BlockSpec((1, T, LANES), lambda n, p: (n, 0, p)),
        out_shape=jax.ShapeDtypeStruct((b, T, da), BF16),
        scratch_shapes=[pltpu.VMEM((T, 2 * LANES), BF16), pltpu.VMEM((2, dh + BF16_ROWS, T), BF16),
                        pltpu.VMEM((T // tq, 2, 2 * LANES, tq), BF16)],
        compiler_params=pltpu.CompilerParams(
            dimension_semantics=("arbitrary", "arbitrary"), vmem_limit_bytes=VMEM_LIMIT),
        name="attn_prompt",
    )(q_t, k_all, v_all, caug)


def _lane_cumsum(x):
    n = x.shape[1]
    lane = lax.broadcasted_iota(jnp.int32, x.shape, 1)
    d = 1
    while d < n:
        x = x + jnp.where(lane >= d, pltpu.roll(x, d, axis=1), 0.0)
        d *= 2
    return x


def _attn_sample_kernel(q_ref, kn_ref, vn_ref, cft_ref, ck_ref, cv_ref, clf_ref, o_ref,
                        *, P, dt, nh, dh):
    cc = _lane_cumsum(clf_ref[...])
    cn = LOG2E * (cc[:, P - 1:P] + cft_ref[0])
    cc = LOG2E * cc
    lane = lax.broadcasted_iota(jnp.int32, (dt, LANES), 1)
    row = lax.broadcasted_iota(jnp.int32, (2 * dt, dt), 0)
    col = lax.broadcasted_iota(jnp.int32, (2 * dt, dt), 1)
    causal = col <= jnp.where(row < dt, row, row - dt)
    for p in range(ck_ref.shape[0] // LANES):
        pr = slice(p * LANES, (p + 1) * LANES)
        kt = ck_ref[pr, :].astype(BF16)
        vt = cv_ref[pr, :].astype(BF16)
        qt = q_ref[0, :, pr]
        q2 = jnp.concatenate([jnp.where(lane < dh, qt, jnp.zeros_like(qt)),
                              jnp.where(lane >= dh, qt, jnp.zeros_like(qt))], axis=0)

        def per_head(c, width):
            return jnp.concatenate([jnp.broadcast_to(c[2 * p:2 * p + 1, :], (dt, width)),
                                    jnp.broadcast_to(c[2 * p + 1:2 * p + 2, :], (dt, width))], axis=0)

        sc = _dot(q2, kt) - per_head(cc, P)
        sn = _dot_nt(q2, kn_ref[0, :, pr].astype(BF16)) - per_head(cn, dt)
        sn = jnp.where(causal, sn, -jnp.inf)
        m = jnp.maximum(jnp.max(sc, axis=1, keepdims=True), jnp.max(sn, axis=1, keepdims=True))
        pc = jnp.exp2(sc - m)
        pn = jnp.exp2(sn - m)
        lsum = jnp.sum(pc, axis=1, keepdims=True) + jnp.sum(pn, axis=1, keepdims=True)
        o = _dot_nt(pc.astype(BF16), vt) + _dot(pn.astype(BF16), vn_ref[0, :, pr].astype(BF16))
        o = o / lsum
        o_ref[0, :, pr] = jnp.where(lane < dh, o[0:dt, :], o[dt:2 * dt, :]).astype(BF16)


def _attn_sample(q, k_new, v_new, cft, cache_kt, cache_vt, cache_lft, l, nh, dh):
    db, dt, da = q.shape
    P = cache_kt.shape[3]
    assert dt % BF16_ROWS == 0
    tok = lambda width: pl.BlockSpec((1, dt, width), lambda n: (n, 0, 0))
    return pl.pallas_call(
        functools.partial(_attn_sample_kernel, P=P, dt=dt, nh=nh, dh=dh),
        grid=(db,),
        in_specs=[
            tok(da), tok(da), tok(da),
            pl.BlockSpec((1, nh, dt), lambda n: (n, 0, 0)),
            pl.BlockSpec((None, None, da, P), lambda n: (l, n, 0, 0)),
            pl.BlockSpec((None, None, da, P), lambda n: (l, n, 0, 0)),
            pl.BlockSpec((None, None, nh, P), lambda n: (l, n, 0, 0)),
        ],
        out_specs=tok(da),
        out_shape=jax.ShapeDtypeStruct((db, dt, da), BF16),
        compiler_params=pltpu.CompilerParams(
            dimension_semantics=("arbitrary",), vmem_limit_bytes=VMEM_LIMIT),
        name="attn_sample",
    )(q, k_new, v_new, cft, cache_kt, cache_vt, cache_lft)


def _ffn_kernel(x_ref, at_ref, cv_ref, gst_ref, woa_ref, woc_ref, g1_ref, b1_ref,
                wug_ref, wuv_ref, fw_ref, fb_ref, wdn_ref, g2_ref, b2_ref,
                y_ref, gout_ref, gbuf, hbuf, a_scr, *, s, tm, hdr, fc, alpha):
    t = pl.program_id(1)

    @pl.when(t == 0)
    def _():
        hbuf[...] = gst_ref[0]

    rs = tm // FFN_SPLITS
    groups = [slice(h * rs, (h + 1) * rs) for h in range(FFN_SPLITS)]
    mixed = [_dot(at_ref[0, r, :], woa_ref[...]) + _dot(cv_ref[0, r, :], woc_ref[...]) for r in groups]
    x1 = [_layer_norm(alpha * x_ref[0, r, :] + mx, g1_ref[...], b1_ref[...])
          for r, mx in zip(groups, mixed)]
    x1b = [v.astype(BF16) for v in x1]

    fw = fw_ref[...]
    fb = fb_ref[...]
    for c in range(wug_ref.shape[1] // fc):
        cs = slice(c * fc, (c + 1) * fc)
        gb = gbuf.at[c % 2]
        gb[hdr - 2 * s:hdr, :] = hbuf[:, cs]
        for h, r in enumerate(groups):
            g = _dot(x1b[h], wug_ref[:, cs])
            val = _dot(x1b[h], wuv_ref[:, cs])
            lo = hdr + h * rs
            gb[lo:lo + rs, :] = g
            g1 = gb[lo - s:lo - s + rs, :]
            g2 = gb[lo - 2 * s:lo - 2 * s + rs, :]
            y = fb[:, cs] + fw[0:1, cs] * g2
            y = y + fw[1:2, cs] * g1
            y = y + fw[2:3, cs] * g
            a = y * (1.0 / (1.0 + jnp.exp(-y))) * val
            a_scr[r, cs] = a.astype(BF16)
        hbuf[:, cs] = gb[hdr + tm - 2 * s:hdr + tm, :]
    for h, r in enumerate(groups):
        down = _dot(a_scr[r, :], wdn_ref[...])
        y_ref[0, r, :] = _layer_norm(alpha * x1[h] + down, g2_ref[...], b2_ref[...])
    gout_ref[0] = hbuf[...]


def _ffn(x, attn, conv, gst, w, l, s, alpha):
    ns, L, D = x.shape
    da = attn.shape[2]
    dc = conv.shape[2]
    dff = w["wug"].shape[2]
    fc = FF_CHUNK
    tm = min(FFN_ROW_TILE, L)
    nt = L // tm
    assert L % tm == 0 and (tm // FFN_SPLITS) % max(s, SUBLANES) == 0
    hdr = max(SUBLANES, 2 * s)

    def wspec(shape):
        idx = (l,) + (0,) * len(shape)
        return pl.BlockSpec((None,) + shape, lambda n, t: idx, pipeline_mode=pl.Buffered(1))

    row = lambda width: pl.BlockSpec((1, tm, width), lambda n, t: (n, t, 0))
    return pl.pallas_call(
        functools.partial(_ffn_kernel, s=s, tm=tm, hdr=hdr, fc=fc, alpha=alpha),
        grid=(ns, nt),
        in_specs=[
            row(D), row(da), row(dc),
            pl.BlockSpec((1, 2 * s, dff), lambda n, t: (n, 0, 0)),
            wspec((da, D)), wspec((dc, D)), wspec((1, D)), wspec((1, D)),
            wspec((D, dff)), wspec((D, dff)), wspec((3, dff)), wspec((1, dff)),
            wspec((dff, D)), wspec((1, D)), wspec((1, D)),
        ],
        out_specs=[row(D), pl.BlockSpec((1, 2 * s, dff), lambda n, t: (n, 0, 0))],
        out_shape=[jax.ShapeDtypeStruct((ns, L, D), F32), jax.ShapeDtypeStruct((ns, 2 * s, dff), F32)],
        scratch_shapes=[pltpu.VMEM((2, hdr + tm, fc), F32), pltpu.VMEM((2 * s, dff), F32),
                        pltpu.VMEM((tm, dff), BF16)],
        compiler_params=pltpu.CompilerParams(
            dimension_semantics=("arbitrary", "arbitrary"), vmem_limit_bytes=VMEM_LIMIT),
        name="ffn",
    )(x, attn, conv, gst, w["woa"], w["woc"], w["g1"], w["b1"], w["wug"], w["wuv"],
      w["fw"], w["fb"], w["wdn"], w["g2"], w["b2"])


def _prep_weights(w_in, b_f, conv_w, conv_b, w_out, ln1_g, ln1_b, w_up, ffn_conv_w, ffn_conv_b,
                  w_down, ln2_g, ln2_b, dh):
    depth, D, _ = w_in.shape
    nh = b_f.shape[1]
    da = nh * dh
    dc = conv_w.shape[2]
    dff = ffn_conv_w.shape[2]
    assert dh == 64 and da % LANES == 0 and 3 * nh <= LANES and dff % FF_CHUNK == 0
    o = 3 * da + nh
    pad = LANES - 3 * nh
    wf = jnp.pad(jnp.tile(w_in[:, :, 3 * da:o], (1, 1, 3)), ((0, 0), (0, 0), (0, pad)))
    bf = jnp.pad(jnp.tile(b_f, (1, 3)), ((0, 0), (0, pad)))[:, None, :]
    return dict(
        nh=nh,
        wq=(w_in[:, :, 0:da] * (dh ** -0.5 * LOG2E)).astype(BF16),
        wk=w_in[:, :, da:2 * da].astype(BF16),
        wv=w_in[:, :, 2 * da:3 * da].astype(BF16),
        wf=wf.astype(BF16), bf=bf,
        wb=w_in[:, :, o:o + dc].astype(BF16),
        wc=w_in[:, :, o + dc:o + 2 * dc].astype(BF16),
        wh=w_in[:, :, o + 2 * dc:o + 3 * dc].astype(BF16),
        cw=conv_w, cb=conv_b[:, None, :],
        woa=w_out[:, 0:da].astype(BF16), woc=w_out[:, da:].astype(BF16),
        g1=ln1_g[:, None, :], b1=ln1_b[:, None, :],
        wug=w_up[:, :, 0:dff].astype(BF16), wuv=w_up[:, :, dff:].astype(BF16),
        fw=ffn_conv_w, fb=ffn_conv_b[:, None, :],
        wdn=w_down.astype(BF16), g2=ln2_g[:, None, :], b2=ln2_b[:, None, :],
    )


def kernel(x_prompt, x_sample, cache_k, cache_v, cache_logf, state_mix_conv, state_ffn_conv, w_in, b_f, conv_w, conv_b, w_out, ln1_g, ln1_b, w_up, ffn_conv_w, ffn_conv_b, w_down, ln2_g, ln2_b):
    depth = w_in.shape[0]
    B, T, D = x_prompt.shape
    DB, DT, _ = x_sample.shape
    P, nh, dh = cache_k.shape[2:]
    da = nh * dh
    dc = conv_w.shape[2]
    dff = ffn_conv_w.shape[2]
    alpha = (2 * depth) ** 0.25
    assert DB % SUBLANES == 0
    w = _prep_weights(w_in, b_f, conv_w, conv_b, w_out, ln1_g, ln1_b, w_up, ffn_conv_w, ffn_conv_b,
                      w_down, ln2_g, ln2_b, dh)

    ckt = cache_k.transpose(0, 1, 3, 4, 2).reshape(depth, DB, da, P)
    cvt = cache_v.transpose(0, 1, 3, 4, 2).reshape(depth, DB, da, P)
    clft = cache_logf.transpose(0, 1, 3, 2)

    def time_major(a):
        return a.transpose(0, 2, 1, 3).reshape(depth, 1, a.shape[2] * DB, a.shape[3])

    mix_tm = time_major(state_mix_conv)
    ffn_tm = time_major(state_ffn_conv)
    zero_mix = jnp.zeros((B, 2, dc), F32)
    zero_ffn = jnp.zeros((B, 2, dff), F32)

    yp = x_prompt
    ys = x_sample.transpose(1, 0, 2).reshape(1, DT * DB, D)
    kvp = kvs = None
    logf_p, mix_p, ffn_p, logf_s, mix_s, ffn_s = [], [], [], [], [], []
    for l in range(depth):
        q, kp, vp, lf, caug, conv, ust = _proj(yp, zero_mix, w, l, 1, kvp, True)
        kvp = (kp, vp)
        attn = _attn_prompt(q, kp, vp, caug, l, nh, dh)
        yp, gst = _ffn(yp, attn, conv, zero_ffn, w, l, 1, alpha)
        logf_p.append(lf); mix_p.append(ust); ffn_p.append(gst)

        q, ks, vs, lf, _, conv, ust, cf = _proj(ys, mix_tm[l], w, l, DB, kvs, False)
        kvs = (ks, vs)

        def batch_major(a):
            return a.reshape(-1, DB, a.shape[-1]).transpose(1, 0, 2)

        cft = batch_major(cf)[:, :, :nh].transpose(0, 2, 1)
        attn = _attn_sample(batch_major(q), batch_major(ks[l]), batch_major(vs[l]), cft,
                            ckt, cvt, clft, l, nh, dh)
        attn = attn.transpose(1, 0, 2).reshape(1, DT * DB, da)
        ys, gst = _ffn(ys, attn, conv, ffn_tm[l], w, l, DB, alpha)
        logf_s.append(batch_major(lf)); mix_s.append(batch_major(ust)); ffn_s.append(batch_major(gst))

    k_s = kvs[0].reshape(depth, DT, DB, nh, dh).transpose(0, 2, 1, 3, 4)
    v_s = kvs[1].reshape(depth, DT, DB, nh, dh).transpose(0, 2, 1, 3, 4)
    return (yp, ys.reshape(DT, DB, D).transpose(1, 0, 2),
            kvp[0].reshape(depth, B, nh, dh, T).transpose(0, 1, 4, 2, 3),
            kvp[1].reshape(depth, B, nh, dh, T).transpose(0, 1, 4, 2, 3),
            jnp.stack(logf_p).transpose(0, 1, 3, 2), jnp.stack(mix_p), jnp.stack(ffn_p),
            k_s, v_s, jnp.stack(logf_s), jnp.stack(mix_s), jnp.stack(ffn_s))
```

```python
import functools

import jax
import jax.numpy as jnp
from jax import lax
from jax.experimental import pallas as pl
from jax.experimental.pallas import tpu as pltpu

F32 = jnp.float32
BF16 = jnp.bfloat16

LANES = 128
SUBLANES = 8
BF16_ROWS = 16
LN_EPS = 1e-5
LOG2E = 1.4426950408889634
ROW_TILE = 1024
ROW_SPLITS = 4
FFN_ROW_TILE = 1024
FFN_SPLITS = 4
Q_TILE = 256
FF_CHUNK = 256
QK_AHEAD = 5
KV_BLOCKS = 1
VMEM_LIMIT = 56 * 1024 * 1024


def _dot(a, b):
    return jnp.dot(a, b, preferred_element_type=F32)


def _dot_nt(a, b):
    return lax.dot_general(a, b, (((1,), (1,)), ((), ())), preferred_element_type=F32)


def _stride_cumsum(x, s):
    rows = x.shape[0]
    row = lax.broadcasted_iota(jnp.int32, x.shape, 0)
    d = s
    while d < rows:
        x = x + jnp.where(row >= d, pltpu.roll(x, d, axis=0), 0.0)
        d *= 2
    return x


def _split3(v, nh):
    hi = v.astype(BF16).astype(F32)
    r1 = v - hi
    mid = r1.astype(BF16).astype(F32)
    lo = (r1 - mid).astype(BF16).astype(F32)
    lane = lax.broadcasted_iota(jnp.int32, v.shape, 1)
    out = jnp.where(lane < nh, hi, jnp.where(lane < 2 * nh, mid, jnp.where(lane < 3 * nh, lo, 0.0)))
    return out.astype(BF16)


def _layer_norm(x, g, b):
    mu = jnp.mean(x, axis=-1, keepdims=True)
    xc = x - mu
    var = jnp.mean(xc * xc, axis=-1, keepdims=True)
    return xc * lax.rsqrt(var + LN_EPS) * g + b


def _head_query_t(qt, hh, head, nh, dh):
    row = lax.broadcasted_iota(jnp.int32, qt.shape, 0)
    qm = jnp.where((row >= hh * dh) & (row < (hh + 1) * dh), qt, jnp.zeros_like(qt))
    ones = jnp.where((row == head) | (row == head + nh) | (row == head + 2 * nh), 1.0, 0.0)
    return jnp.concatenate([qm, ones.astype(BF16)], axis=0)


def _proj_kernel(*refs, s, tm, hdr, nh, aliased, kv_t):
    (x_ref, st_ref, wq_ref, wk_ref, wv_ref, wf_ref, wb_ref, wc_ref, wh_ref,
     bf_ref, cw_ref, cb_ref) = refs[:12]
    rest = refs[14:] if aliased else refs[12:]
    (q_ref, k_ref, v_ref, logf_ref, caug_ref, conv_ref, ust_ref) = rest[:7]
    cf_ref = None if kv_t else rest[7]
    ubuf, ccar = rest[-2:]
    t = pl.program_id(1)

    @pl.when(t == 0)
    def _():
        ccar[...] = jnp.zeros_like(ccar)
        ubuf[hdr - 2 * s:hdr, :] = st_ref[0]

    cw = cw_ref[...]
    rs = tm // ROW_SPLITS
    for h in range(ROW_SPLITS):
        r = slice(h * rs, (h + 1) * rs)
        xb = x_ref[0, r, :].astype(BF16)

        fl = _dot(xb, wf_ref[...]) + bf_ref[...]
        logf = jnp.minimum(fl, 0.0) - jnp.log1p(jnp.exp(-jnp.abs(fl)))
        if kv_t:
            logf_ref[0, :, r] = logf.T[:nh, :]
        else:
            logf_ref[0, r, :] = logf[:, :nh]
        cs = _stride_cumsum(logf, s)
        if s == 1:
            c = cs + ccar[0:1, :]
            ccar[...] = jnp.broadcast_to(c[rs - 1:rs, :], ccar.shape)
        else:
            c = cs + jnp.concatenate([ccar[...]] * (rs // s), axis=0)
            ccar[...] = c[rs - s:rs, :]
        if cf_ref is not None:
            cf_ref[0, r, :] = c
        caug_ref[0, r, :] = _split3(-LOG2E * c, nh)

        gb = _dot(xb, wb_ref[...])
        u = _dot(xb, wc_ref[...]) * _dot(xb, wh_ref[...])
        lo = hdr + h * rs
        ubuf[lo:lo + rs, :] = u
        u1 = ubuf[lo - s:lo - s + rs, :]
        u2 = ubuf[lo - 2 * s:lo - 2 * s + rs, :]
        conv = cb_ref[...] + cw[0:1, :] * u2
        conv = conv + cw[1:2, :] * u1
        conv = conv + cw[2:3, :] * u
        conv_ref[0, r, :] = (gb * conv).astype(BF16)

        q = _dot(xb, wq_ref[...])
        k = _dot(xb, wk_ref[...])
        v = _dot(xb, wv_ref[...])
        if kv_t:
            q_ref[0, :, r] = q.T.astype(BF16)
            k_ref[0, 0, :, r] = k.T
            v_ref[0, 0, :, r] = v.T
        else:
            q_ref[0, r, :] = q.astype(BF16)
            k_ref[0, 0, r, :] = k
            v_ref[0, 0, r, :] = v
    tail = ubuf[hdr + tm - 2 * s:hdr + tm, :]
    ust_ref[0] = tail
    ubuf[hdr - 2 * s:hdr, :] = tail


def _proj(x, st, w, l, s, kv_bufs, kv_t):
    ns, L, D = x.shape
    depth = w["wq"].shape[0]
    da = w["wq"].shape[2]
    dc = w["wb"].shape[2]
    nh = w["nh"]
    tm = min(ROW_TILE, L)
    nt = L // tm
    assert L % tm == 0 and tm % ROW_SPLITS == 0 and (tm // ROW_SPLITS) % max(s, SUBLANES) == 0
    hdr = max(SUBLANES, 2 * s)
    aliased = kv_bufs is not None

    def wspec(shape):
        return pl.BlockSpec((None,) + shape, lambda n, t: (l, 0, 0), pipeline_mode=pl.Buffered(1))

    in_specs = [
        pl.BlockSpec((1, tm, D), lambda n, t: (n, t, 0)),
        pl.BlockSpec((1, 2 * s, dc), lambda n, t: (n, 0, 0)),
        wspec((D, da)), wspec((D, da)), wspec((D, da)), wspec((D, LANES)),
        wspec((D, dc)), wspec((D, dc)), wspec((D, dc)),
        wspec((1, LANES)), wspec((3, dc)), wspec((1, dc)),
    ]
    args = [x, st, w["wq"], w["wk"], w["wv"], w["wf"], w["wb"], w["wc"], w["wh"],
            w["bf"], w["cw"], w["cb"]]
    io_alias = {}
    if aliased:
        in_specs += [pl.BlockSpec(memory_space=pl.ANY), pl.BlockSpec(memory_space=pl.ANY)]
        args += list(kv_bufs)
        io_alias = {12: 1, 13: 2}
    if kv_t:
        q_shape = jax.ShapeDtypeStruct((ns, da, L), BF16)
        q_spec = pl.BlockSpec((1, da, tm), lambda n, t: (n, 0, t))
        kv_shape = jax.ShapeDtypeStruct((depth, ns, da, L), F32)
        kv_spec = pl.BlockSpec((1, 1, da, tm), lambda n, t: (l, n, 0, t))
        lf_shape = jax.ShapeDtypeStruct((ns, nh, L), F32)
        lf_spec = pl.BlockSpec((1, nh, tm), lambda n, t: (n, 0, t))
    else:
        q_shape = jax.ShapeDtypeStruct((ns, L, da), BF16)
        q_spec = pl.BlockSpec((1, tm, da), lambda n, t: (n, t, 0))
        kv_shape = jax.ShapeDtypeStruct((depth, ns, L, da), F32)
        kv_spec = pl.BlockSpec((1, 1, tm, da), lambda n, t: (l, n, t, 0))
        lf_shape = jax.ShapeDtypeStruct((ns, L, nh), F32)
        lf_spec = pl.BlockSpec((1, tm, nh), lambda n, t: (n, t, 0))
    out_shape = [
        q_shape, kv_shape, kv_shape, lf_shape,
        jax.ShapeDtypeStruct((ns, L, LANES), BF16),
        jax.ShapeDtypeStruct((ns, L, dc), BF16),
        jax.ShapeDtypeStruct((ns, 2 * s, dc), F32),
    ]
    out_specs = [
        q_spec, kv_spec, kv_spec, lf_spec,
        pl.BlockSpec((1, tm, LANES), lambda n, t: (n, t, 0)),
        pl.BlockSpec((1, tm, dc), lambda n, t: (n, t, 0)),
        pl.BlockSpec((1, 2 * s, dc), lambda n, t: (n, 0, 0)),
    ]
    if not kv_t:
        out_shape.append(jax.ShapeDtypeStruct((ns, L, LANES), F32))
        out_specs.append(pl.BlockSpec((1, tm, LANES), lambda n, t: (n, t, 0)))
    return pl.pallas_call(
        functools.partial(_proj_kernel, s=s, tm=tm, hdr=hdr, nh=nh, aliased=aliased, kv_t=kv_t),
        grid=(ns, nt),
        in_specs=in_specs, out_specs=out_specs, out_shape=out_shape,
        scratch_shapes=[pltpu.VMEM((hdr + tm, dc), F32), pltpu.VMEM((max(s, SUBLANES), LANES), F32)],
        input_output_aliases=io_alias,
        compiler_params=pltpu.CompilerParams(
            dimension_semantics=("arbitrary", "arbitrary"), vmem_limit_bytes=VMEM_LIMIT),
        name="proj",
    )(*args)


def _attn_prompt_kernel(q_ref, k_ref, v_ref, caug_ref, o_ref, kaug, vt, qa_scr, *, T, tq, nh, dh):
    p = pl.program_id(1)
    kaug[:, 0:LANES] = k_ref[0, 0].T.astype(BF16)
    kaug[:, LANES:2 * LANES] = caug_ref[0]
    for hh in range(2):
        vt[hh, 0:dh, :] = v_ref[0, 0, hh * dh:(hh + 1) * dh, :].astype(BF16)
        vt[hh, dh:, :] = jnp.ones((vt.shape[1] - dh, T), BF16)
    row = lax.broadcasted_iota(jnp.int32, (tq, tq), 0)
    col = lax.broadcasted_iota(jnp.int32, (tq, tq), 1)
    causal = row <= col
    nq = T // tq
    for c in range(nq):
        qt = q_ref[0, :, c * tq:(c + 1) * tq]
        for hh in range(2):
            qa_scr[c, hh] = _head_query_t(qt, hh, 2 * p + hh, nh, dh)
    streams = [[(c, hh, r0, min(r0 + KV_BLOCKS, c + 1)) for r0 in range(0, c + 1, KV_BLOCKS)]
               for c in range(nq) for hh in range(2)]
    units = [st[r] for r in range(max(map(len, streams))) for st in streams if r < len(st)]

    def scores(u):
        c, hh, r0, r1 = u
        return _dot(kaug[r0 * tq:r1 * tq, :], qa_scr[c, hh])

    pending = {k: scores(units[k]) for k in range(min(QK_AHEAD, len(units)))}
    state, outs = {}, {}
    for k, (c, hh, r0, r1) in enumerate(units):
        s = pending.pop(k)
        if k + QK_AHEAD < len(units):
            pending[k + QK_AHEAD] = scores(units[k + QK_AHEAD])
        last = r1 == c + 1
        if last:
            sd = jnp.where(causal, s[-tq:, :], -jnp.inf)
            s = sd if r1 - r0 == 1 else jnp.concatenate([s[:-tq, :], sd], axis=0)
        vj = vt[hh, :, r0 * tq:r1 * tq]
        mc = jnp.max(s, axis=0, keepdims=True)
        if r0 == 0:
            m = mc
            acc = _dot(vj, jnp.exp2(s - m).astype(BF16))
        else:
            m0, a0 = state.pop((c, hh))
            m = jnp.maximum(m0, mc)
            acc = jnp.exp2(m0 - m) * a0 + _dot(vj, jnp.exp2(s - m).astype(BF16))
        if last:
            outs[hh] = acc[0:dh, :] / acc[dh:dh + 1, :]
            if hh == 1:
                o_ref[0, c * tq:(c + 1) * tq, :] = jnp.concatenate(
                    [outs[0], outs[1]], axis=0).T.astype(BF16)
        else:
            state[c, hh] = (m, acc)


def _attn_prompt(q_t, k_all, v_all, caug, l, nh, dh):
    b, da, T = q_t.shape
    tq = min(Q_TILE, T)
    npair = da // LANES
    return pl.pallas_call(
        functools.partial(_attn_prompt_kernel, T=T, tq=tq, nh=nh, dh=dh),
        grid=(b, npair),
        in_specs=[
            pl.BlockSpec((1, LANES, T), lambda n, p: (n, p, 0)),
            pl.BlockSpec((1, 1, LANES, T), lambda n, p: (l, n, p, 0)),
            pl.BlockSpec((1, 1, LANES, T), lambda n, p: (l, n, p, 0)),
            pl.BlockSpec((1, T, LANES), lambda n, p: (n, 0, 0)),
        ],
        out_specs=pl.BlockSpec((1, T, LANES), lambda n, p: (n, 0, p)),
        out_shape=jax.ShapeDtypeStruct((b, T, da), BF16),
        scratch_shapes=[pltpu.VMEM((T, 2 * LANES), BF16), pltpu.VMEM((2, dh + BF16_ROWS, T), BF16),
                        pltpu.VMEM((T // tq, 2, 2 * LANES, tq), BF16)],
        compiler_params=pltpu.CompilerParams(
            dimension_semantics=("arbitrary", "arbitrary"), vmem_limit_bytes=VMEM_LIMIT),
        name="attn_prompt",
    )(q_t, k_all, v_all, caug)


def _lane_cumsum(x):
    n = x.shape[1]
    lane = lax.broadcasted_iota(jnp.int32, x.shape, 1)
    d = 1
    while d < n:
        x = x + jnp.where(lane >= d, pltpu.roll(x, d, axis=1), 0.0)
        d *= 2
    return x


def _attn_sample_kernel(q_ref, kn_ref, vn_ref, cft_ref, ck_ref, cv_ref, clf_ref, o_ref,
                        *, P, dt, nh, dh):
    cc = _lane_cumsum(clf_ref[...])
    cn = LOG2E * (cc[:, P - 1:P] + cft_ref[0])
    cc = LOG2E * cc
    lane = lax.broadcasted_iota(jnp.int32, (dt, LANES), 1)
    row = lax.broadcasted_iota(jnp.int32, (2 * dt, dt), 0)
    col = lax.broadcasted_iota(jnp.int32, (2 * dt, dt), 1)
    causal = col <= jnp.where(row < dt, row, row - dt)
    for p in range(ck_ref.shape[0] // LANES):
        pr = slice(p * LANES, (p + 1) * LANES)
        kt = ck_ref[pr, :].astype(BF16)
        vt = cv_ref[pr, :].astype(BF16)
        qt = q_ref[0, :, pr]
        q2 = jnp.concatenate([jnp.where(lane < dh, qt, jnp.zeros_like(qt)),
                              jnp.where(lane >= dh, qt, jnp.zeros_like(qt))], axis=0)

        def per_head(c, width):
            return jnp.concatenate([jnp.broadcast_to(c[2 * p:2 * p + 1, :], (dt, width)),
                                    jnp.broadcast_to(c[2 * p + 1:2 * p + 2, :], (dt, width))], axis=0)

        sc = _dot(q2, kt) - per_head(cc, P)
        sn = _dot_nt(q2, kn_ref[0, :, pr].astype(BF16)) - per_head(cn, dt)
        sn = jnp.where(causal, sn, -jnp.inf)
        m = jnp.maximum(jnp.max(sc, axis=1, keepdims=True), jnp.max(sn, axis=1, keepdims=True))
        pc = jnp.exp2(sc - m)
        pn = jnp.exp2(sn - m)
        lsum = jnp.sum(pc, axis=1, keepdims=True) + jnp.sum(pn, axis=1, keepdims=True)
        o = _dot_nt(pc.astype(BF16), vt) + _dot(pn.astype(BF16), vn_ref[0, :, pr].astype(BF16))
        o = o / lsum
        o_ref[0, :, pr] = jnp.where(lane < dh, o[0:dt, :], o[dt:2 * dt, :]).astype(BF16)


def _attn_sample(q, k_new, v_new, cft, cache_kt, cache_vt, cache_lft, l, nh, dh):
    db, dt, da = q.shape
    P = cache_kt.shape[3]
    assert dt % BF16_ROWS == 0
    tok = lambda width: pl.BlockSpec((1, dt, width), lambda n: (n, 0, 0))
    return pl.pallas_call(
        functools.partial(_attn_sample_kernel, P=P, dt=dt, nh=nh, dh=dh),
        grid=(db,),
        in_specs=[
            tok(da), tok(da), tok(da),
            pl.BlockSpec((1, nh, dt), lambda n: (n, 0, 0)),
            pl.BlockSpec((None, None, da, P), lambda n: (l, n, 0, 0)),
            pl.BlockSpec((None, None, da, P), lambda n: (l, n, 0, 0)),
            pl.BlockSpec((None, None, nh, P), lambda n: (l, n, 0, 0)),
        ],
        out_specs=tok(da),
        out_shape=jax.ShapeDtypeStruct((db, dt, da), BF16),
        compiler_params=pltpu.CompilerParams(
            dimension_semantics=("arbitrary",), vmem_limit_bytes=VMEM_LIMIT),
        name="attn_sample",
    )(q, k_new, v_new, cft, cache_kt, cache_vt, cache_lft)


def _ffn_kernel(x_ref, at_ref, cv_ref, gst_ref, woa_ref, woc_ref, g1_ref, b1_ref,
                wug_ref, wuv_ref, fw_ref, fb_ref, wdn_ref, g2_ref, b2_ref,
                y_ref, gout_ref, gbuf, hbuf, a_scr, *, s, tm, hdr, fc, alpha):
    t = pl.program_id(1)

    @pl.when(t == 0)
    def _():
        hbuf[...] = gst_ref[0]

    rs = tm // FFN_SPLITS
    groups = [slice(h * rs, (h + 1) * rs) for h in range(FFN_SPLITS)]
    mixed = [_dot(at_ref[0, r, :], woa_ref[...]) + _dot(cv_ref[0, r, :], woc_ref[...]) for r in groups]
    x1 = [_layer_norm(alpha * x_ref[0, r, :] + mx, g1_ref[...], b1_ref[...])
          for r, mx in zip(groups, mixed)]
    x1b = [v.astype(BF16) for v in x1]

    fw = fw_ref[...]
    fb = fb_ref[...]
    for c in range(wug_ref.shape[1] // fc):
        cs = slice(c * fc, (c + 1) * fc)
        gb = gbuf.at[c % 2]
        gb[hdr - 2 * s:hdr, :] = hbuf[:, cs]
        for h, r in enumerate(groups):
            g = _dot(x1b[h], wug_ref[:, cs])
            val = _dot(x1b[h], wuv_ref[:, cs])
            lo = hdr + h * rs
            gb[lo:lo + rs, :] = g
            g1 = gb[lo - s:lo - s + rs, :]
            g2 = gb[lo - 2 * s:lo - 2 * s + rs, :]
            y = fb[:, cs] + fw[0:1, cs] * g2
            y = y + fw[1:2, cs] * g1
            y = y + fw[2:3, cs] * g
            a = y * (1.0 / (1.0 + jnp.exp(-y))) * val
            a_scr[r, cs] = a.astype(BF16)
        hbuf[:, cs] = gb[hdr + tm - 2 * s:hdr + tm, :]
    for h, r in enumerate(groups):
        down = _dot(a_scr[r, :], wdn_ref[...])
        y_ref[0, r, :] = _layer_norm(alpha * x1[h] + down, g2_ref[...], b2_ref[...])
    gout_ref[0] = hbuf[...]


def _ffn(x, attn, conv, gst, w, l, s, alpha):
    ns, L, D = x.shape
    da = attn.shape[2]
    dc = conv.shape[2]
    dff = w["wdn"].shape[1]
    fc = FF_CHUNK
    tm = min(FFN_ROW_TILE, L)
    nt = L // tm
    assert L % tm == 0 and (tm // FFN_SPLITS) % max(s, SUBLANES) == 0
    hdr = max(SUBLANES, 2 * s)

    def wspec(shape, blk=(0, 0)):
        idx = (l,) + blk
        return pl.BlockSpec((None,) + shape, lambda n, t: idx, pipeline_mode=pl.Buffered(1))

    row = lambda width: pl.BlockSpec((1, tm, width), lambda n, t: (n, t, 0))
    return pl.pallas_call(
        functools.partial(_ffn_kernel, s=s, tm=tm, hdr=hdr, fc=fc, alpha=alpha),
        grid=(ns, nt),
        in_specs=[
            row(D), row(da), row(dc),
            pl.BlockSpec((1, 2 * s, dff), lambda n, t: (n, 0, 0)),
            wspec((da, D)), wspec((dc, D), (1, 0)), wspec((1, D)), wspec((1, D)),
            wspec((D, dff)), wspec((D, dff), (0, 1)), wspec((3, dff)), wspec((1, dff)),
            wspec((dff, D)), wspec((1, D)), wspec((1, D)),
        ],
        out_specs=[row(D), pl.BlockSpec((1, 2 * s, dff), lambda n, t: (n, 0, 0))],
        out_shape=[jax.ShapeDtypeStruct((ns, L, D), F32), jax.ShapeDtypeStruct((ns, 2 * s, dff), F32)],
        scratch_shapes=[pltpu.VMEM((2, hdr + tm, fc), F32), pltpu.VMEM((2 * s, dff), F32),
                        pltpu.VMEM((tm, dff), BF16)],
        compiler_params=pltpu.CompilerParams(
            dimension_semantics=("arbitrary", "arbitrary"), vmem_limit_bytes=VMEM_LIMIT),
        name="ffn",
    )(x, attn, conv, gst, w["wo"], w["wo"], w["g1"], w["b1"], w["wu"], w["wu"],
      w["fw"], w["fb"], w["wdn"], w["g2"], w["b2"])


def _prep_weights(w_in, b_f, conv_w, conv_b, w_out, ln1_g, ln1_b, w_up, ffn_conv_w, ffn_conv_b,
                  w_down, ln2_g, ln2_b, dh):
    depth, D, _ = w_in.shape
    nh = b_f.shape[1]
    da = nh * dh
    dc = conv_w.shape[2]
    dff = ffn_conv_w.shape[2]
    assert dh == 64 and da % LANES == 0 and 3 * nh <= LANES and dff % FF_CHUNK == 0
    o = 3 * da + nh
    pad = LANES - 3 * nh
    wf = jnp.pad(jnp.tile(w_in[:, :, 3 * da:o], (1, 1, 3)), ((0, 0), (0, 0), (0, pad)))
    bf = jnp.pad(jnp.tile(b_f, (1, 3)), ((0, 0), (0, pad)))[:, None, :]
    return dict(
        nh=nh,
        wq=(w_in[:, :, 0:da] * (dh ** -0.5 * LOG2E)).astype(BF16),
        wk=w_in[:, :, da:2 * da].astype(BF16),
        wv=w_in[:, :, 2 * da:3 * da].astype(BF16),
        wf=wf.astype(BF16), bf=bf,
        wb=w_in[:, :, o:o + dc].astype(BF16),
        wc=w_in[:, :, o + dc:o + 2 * dc].astype(BF16),
        wh=w_in[:, :, o + 2 * dc:o + 3 * dc].astype(BF16),
        cw=conv_w, cb=conv_b[:, None, :],
        wo=w_out.astype(BF16),
        g1=ln1_g[:, None, :], b1=ln1_b[:, None, :],
        wu=w_up.astype(BF16),
        fw=ffn_conv_w, fb=ffn_conv_b[:, None, :],
        wdn=w_down.astype(BF16), g2=ln2_g[:, None, :], b2=ln2_b[:, None, :],
    )


def kernel(x_prompt, x_sample, cache_k, cache_v, cache_logf, state_mix_conv, state_ffn_conv, w_in, b_f, conv_w, conv_b, w_out, ln1_g, ln1_b, w_up, ffn_conv_w, ffn_conv_b, w_down, ln2_g, ln2_b):
    depth = w_in.shape[0]
    B, T, D = x_prompt.shape
    DB, DT, _ = x_sample.shape
    P, nh, dh = cache_k.shape[2:]
    da = nh * dh
    dc = conv_w.shape[2]
    dff = ffn_conv_w.shape[2]
    alpha = (2 * depth) ** 0.25
    assert DB % SUBLANES == 0
    w = _prep_weights(w_in, b_f, conv_w, conv_b, w_out, ln1_g, ln1_b, w_up, ffn_conv_w, ffn_conv_b,
                      w_down, ln2_g, ln2_b, dh)

    ckt = cache_k.transpose(0, 1, 3, 4, 2).reshape(depth, DB, da, P)
    cvt = cache_v.transpose(0, 1, 3, 4, 2).reshape(depth, DB, da, P)
    clft = cache_logf.transpose(0, 1, 3, 2)

    def time_major(a):
        return a.transpose(0, 2, 1, 3).reshape(depth, 1, a.shape[2] * DB, a.shape[3])

    mix_tm = time_major(state_mix_conv)
    ffn_tm = time_major(state_ffn_conv)
    zero_mix = jnp.zeros((B, 2, dc), F32)
    zero_ffn = jnp.zeros((B, 2, dff), F32)

    yp = x_prompt
    ys = x_sample.transpose(1, 0, 2).reshape(1, DT * DB, D)
    kvp = kvs = None
    logf_p, mix_p, ffn_p, logf_s, mix_s, ffn_s = [], [], [], [], [], []
    for l in range(depth):
        q, kp, vp, lf, caug, conv, ust = _proj(yp, zero_mix, w, l, 1, kvp, True)
        kvp = (kp, vp)
        attn = _attn_prompt(q, kp, vp, caug, l, nh, dh)
        yp, gst = _ffn(yp, attn, conv, zero_ffn, w, l, 1, alpha)
        logf_p.append(lf); mix_p.append(ust); ffn_p.append(gst)

        q, ks, vs, lf, _, conv, ust, cf = _proj(ys, mix_tm[l], w, l, DB, kvs, False)
        kvs = (ks, vs)

        def batch_major(a):
            return a.reshape(-1, DB, a.shape[-1]).transpose(1, 0, 2)

        cft = batch_major(cf)[:, :, :nh].transpose(0, 2, 1)
        attn = _attn_sample(batch_major(q), batch_major(ks[l]), batch_major(vs[l]), cft,
                            ckt, cvt, clft, l, nh, dh)
        attn = attn.transpose(1, 0, 2).reshape(1, DT * DB, da)
        ys, gst = _ffn(ys, attn, conv, ffn_tm[l], w, l, DB, alpha)
        logf_s.append(batch_major(lf)); mix_s.append(batch_major(ust)); ffn_s.append(batch_major(gst))

    k_s = kvs[0].reshape(depth, DT, DB, nh, dh).transpose(0, 2, 1, 3, 4)
    v_s = kvs[1].reshape(depth, DT, DB, nh, dh).transpose(0, 2, 1, 3, 4)
    return (yp, ys.reshape(DT, DB, D).transpose(1, 0, 2),
            kvp[0].reshape(depth, B, nh, dh, T).transpose(0, 1, 4, 2, 3),
            kvp[1].reshape(depth, B, nh, dh, T).transpose(0, 1, 4, 2, 3),
            jnp.stack(logf_p).transpose(0, 1, 3, 2), jnp.stack(mix_p), jnp.stack(ffn_p),
            k_s, v_s, jnp.stack(logf_s), jnp.stack(mix_s), jnp.stack(ffn_s))
```
